```python
import math
import jax
import jax.numpy as jnp
from jax import lax
import numpy as np


D_MODEL = 2048
BATCH = 4
SEQ = 2048
DEPTH = 4

GDN_HEADS = 8
GDN_HEAD_DIM = 128
GDN_WIDTH = GDN_HEADS * GDN_HEAD_DIM
GDN_CHUNK = 64
QKV_CONV = 5
N_DIR = 2
SGU_GROUPS = 8
SGU_GROUP_DIM = 128
SGU_WIDTH = SGU_GROUPS * SGU_GROUP_DIM
SGU_BLOCK = 128
D_FF = 5632
FFN_CONV = 3
NORM_EPS = 1e-6
IN_SIZES = (3 * GDN_WIDTH, GDN_WIDTH, N_DIR * GDN_HEADS, N_DIR * GDN_HEADS, SGU_WIDTH, SGU_WIDTH, D_MODEL, D_MODEL)
N_IN = 3 * GDN_WIDTH + GDN_WIDTH + 2 * N_DIR * GDN_HEADS + 2 * SGU_WIDTH + 2 * D_MODEL

kernel_name = 'hybrid_gdn_sgu_convglu_encoder'


def rms_norm(x, g):
    xf = x.astype(jnp.float32)
    y = xf * lax.rsqrt(jnp.mean(xf * xf, axis=-1, keepdims=True) + NORM_EPS)
    return (y * g.astype(jnp.float32)).astype(x.dtype)


def l2_normalize(x):
    xf = x.astype(jnp.float32)
    return xf * lax.rsqrt(jnp.sum(xf * xf, axis=-1, keepdims=True) + NORM_EPS)


def depthwise_conv_centred(x, w):
    pad = w.shape[0] // 2
    return lax.conv_general_dilated(
        x, w[:, None, :].astype(x.dtype), window_strides=(1,), padding=[(pad, pad)],
        dimension_numbers=('NWC', 'WIO', 'NWC'), feature_group_count=x.shape[-1])


def split_columns(p):
    offsets = []
    acc = 0
    for s in IN_SIZES[:-1]:
        acc += s
        offsets.append(acc)
    return jnp.split(p, offsets, axis=-1)


def gated_delta_rule_chunked(q, k, v, g, beta):
    bsz, nh, seqlen, dk = k.shape
    dv = v.shape[-1]
    c = GDN_CHUNK
    nc = seqlen // c
    q = q * (dk ** -0.5)
    blk = lambda t: t.reshape(bsz, nh, nc, c, *t.shape[3:])
    q, k, v, g, beta = blk(q), blk(k), blk(v), blk(g), blk(beta)
    g = jnp.cumsum(g, axis=-1)
    lower = jnp.tril(jnp.ones((c, c), dtype=bool))
    decay = jnp.where(lower, jnp.exp(jnp.where(lower, g[..., :, None] - g[..., None, :], 0.0)), 0.0)
    k_beta = k * beta[..., None]
    kk = jnp.einsum('bhnid,bhnjd->bhnij', k_beta, k) * decay
    rhs = jnp.concatenate([v * beta[..., None], k_beta * jnp.exp(g)[..., None]], axis=-1)
    sol = lax.linalg.triangular_solve(kk, rhs, left_side=True, lower=True, unit_diagonal=True)
    u, w = sol[..., :dv], sol[..., dv:]
    qk = jnp.einsum('bhnid,bhnjd->bhnij', q, k) * decay
    q_dec = q * jnp.exp(g)[..., None]
    k_dec = k * jnp.exp(g[..., -1:] - g)[..., None]
    chunk_dec = jnp.exp(g[..., -1])

    def step(state, xs):
        q_i, qk_i, u_i, w_i, k_i, d_i = xs
        v_new = u_i - jnp.einsum('bhck,bhkv->bhcv', w_i, state)
        o_i = jnp.einsum('bhck,bhkv->bhcv', q_i, state) + jnp.einsum('bhcs,bhsv->bhcv', qk_i, v_new)
        state = state * d_i[..., None, None] + jnp.einsum('bhck,bhcv->bhkv', k_i, v_new)
        return state, o_i

    xs = tuple(jnp.moveaxis(t, 2, 0) for t in (q_dec, qk, u, w, k_dec, chunk_dec))
    state0 = jnp.zeros((bsz, nh, dk, dv), jnp.float32)
    _, o = lax.scan(step, state0, xs)
    return jnp.moveaxis(o, 0, 2).reshape(bsz, nh, seqlen, dv)


def bidirectional_gated_deltanet(qkv, z, a, b, conv_w, a_log, dt_bias, norm_g):
    bsz, seqlen, _ = qkv.shape
    f32 = jnp.float32
    qkv_c = jax.nn.silu(depthwise_conv_centred(qkv, conv_w))
    q, k, v = jnp.split(qkv_c, 3, axis=-1)
    heads = lambda t: t.reshape(bsz, seqlen, GDN_HEADS, GDN_HEAD_DIM).transpose(0, 2, 1, 3)
    q, k, v = l2_normalize(heads(q)), l2_normalize(heads(k)), heads(v).astype(f32)
    a = a.astype(f32).reshape(bsz, seqlen, N_DIR, GDN_HEADS).transpose(0, 2, 3, 1)
    b = b.astype(f32).reshape(bsz, seqlen, N_DIR, GDN_HEADS).transpose(0, 2, 3, 1)
    g = -jnp.exp(a_log.astype(f32))[None, :, :, None] * jax.nn.softplus(a + dt_bias.astype(f32)[None, :, :, None])
    beta = jax.nn.sigmoid(b)
    rev = lambda t: jnp.flip(t, axis=2)
    o2 = gated_delta_rule_chunked(
        jnp.concatenate([q, rev(q)], axis=1),
        jnp.concatenate([k, rev(k)], axis=1),
        jnp.concatenate([v, rev(v)], axis=1),
        jnp.concatenate([g[:, 0], rev(g[:, 1])], axis=1),
        jnp.concatenate([beta[:, 0], rev(beta[:, 1])], axis=1))
    o = o2[:, :GDN_HEADS] + rev(o2[:, GDN_HEADS:])
    o = o.transpose(0, 2, 1, 3)
    zg = jax.nn.silu(z.astype(f32)).reshape(bsz, seqlen, GDN_HEADS, GDN_HEAD_DIM)
    y = rms_norm(o, norm_g) * zg
    return y.reshape(bsz, seqlen, GDN_WIDTH).astype(qkv.dtype)


def chunked_spatial_gating(u, v, ln_g, ln_b, w_s, b_s):
    bsz, seqlen, _ = v.shape
    vf = v.astype(jnp.float32)
    mu = jnp.mean(vf, axis=-1, keepdims=True)
    var = jnp.mean(jnp.square(vf - mu), axis=-1, keepdims=True)
    vn = ((vf - mu) * lax.rsqrt(var + NORM_EPS) * ln_g.astype(jnp.float32) + ln_b.astype(jnp.float32)).astype(v.dtype)
    vn = vn.reshape(bsz, seqlen // SGU_BLOCK, SGU_BLOCK, SGU_GROUPS, SGU_GROUP_DIM)
    s = jnp.einsum('gts,bnsgc->bntgc', w_s, vn) + b_s.T[None, None, :, :, None]
    return u * s.reshape(bsz, seqlen, SGU_WIDTH)


def setup_inputs(seed: int = 0) -> dict:
    key = jax.random.key(seed)
    ks = jax.random.split(key, 24)
    f32 = jnp.float32
    nrm = lambda k, shape, scale: jax.random.normal(k, shape, f32) * scale
    x = nrm(ks[0], (BATCH, SEQ, D_MODEL), 1.0)
    norm_mix_g = 1.0 + nrm(ks[1], (DEPTH, D_MODEL), 0.02)
    w_in = nrm(ks[2], (DEPTH, D_MODEL, N_IN), D_MODEL ** -0.5)
    qkv_conv_w = nrm(ks[3], (DEPTH, QKV_CONV, 3 * GDN_WIDTH), QKV_CONV ** -0.5)
    a_log = jnp.log(jax.random.uniform(ks[4], (DEPTH, N_DIR, GDN_HEADS), f32, minval=1.0, maxval=16.0))
    dt = jnp.exp(jax.random.uniform(ks[5], (DEPTH, N_DIR, GDN_HEADS), f32, minval=math.log(1e-3), maxval=math.log(1e-1)))
    dt_bias = dt + jnp.log(-jnp.expm1(-dt))
    gdn_norm_g = 1.0 + nrm(ks[6], (DEPTH, GDN_HEAD_DIM), 0.02)
    w_branch_a = nrm(ks[7], (DEPTH, GDN_WIDTH, D_MODEL), GDN_WIDTH ** -0.5)
    sgu_ln_g = 1.0 + nrm(ks[8], (DEPTH, SGU_WIDTH), 0.02)
    sgu_ln_b = nrm(ks[9], (DEPTH, SGU_WIDTH), 0.02)
    sgu_w = nrm(ks[10], (DEPTH, SGU_GROUPS, SGU_BLOCK, SGU_BLOCK), SGU_BLOCK ** -0.5)
    sgu_b = 1.0 + nrm(ks[11], (DEPTH, SGU_GROUPS, SGU_BLOCK), 0.02)
    w_branch_b = nrm(ks[12], (DEPTH, SGU_WIDTH, D_MODEL), SGU_WIDTH ** -0.5)
    w_out = nrm(ks[13], (DEPTH, D_MODEL, D_MODEL), D_MODEL ** -0.5)
    norm_ffn_g = 1.0 + nrm(ks[14], (DEPTH, D_MODEL), 0.02)
    w_up = nrm(ks[15], (DEPTH, D_MODEL, 2 * D_FF), D_MODEL ** -0.5)
    ffn_conv_w = nrm(ks[16], (DEPTH, FFN_CONV, 2 * D_FF), FFN_CONV ** -0.5)
    ffn_conv_b = nrm(ks[17], (DEPTH, 2 * D_FF), 0.02)
    w_down = nrm(ks[18], (DEPTH, D_FF, D_MODEL), D_FF ** -0.5)
    final_norm_g = 1.0 + nrm(ks[19], (D_MODEL,), 0.02)
    return {'x': x, 'norm_mix_g': norm_mix_g, 'w_in': w_in, 'qkv_conv_w': qkv_conv_w,
            'a_log': a_log, 'dt_bias': dt_bias, 'gdn_norm_g': gdn_norm_g, 'w_branch_a': w_branch_a,
            'sgu_ln_g': sgu_ln_g, 'sgu_ln_b': sgu_ln_b, 'sgu_w': sgu_w, 'sgu_b': sgu_b,
            'w_branch_b': w_branch_b, 'w_out': w_out, 'norm_ffn_g': norm_ffn_g, 'w_up': w_up,
            'ffn_conv_w': ffn_conv_w, 'ffn_conv_b': ffn_conv_b, 'w_down': w_down,
            'final_norm_g': final_norm_g}


def reference(x, norm_mix_g, w_in, qkv_conv_w, a_log, dt_bias, gdn_norm_g, w_branch_a,
              sgu_ln_g, sgu_ln_b, sgu_w, sgu_b, w_branch_b, w_out, norm_ffn_g, w_up,
              ffn_conv_w, ffn_conv_b, w_down, final_norm_g):
    for l in range(DEPTH):
        h = rms_norm(x, norm_mix_g[l])
        qkv, z, a, b, u, v, gate_a, gate_b = split_columns(h @ w_in[l])
        y_a = bidirectional_gated_deltanet(qkv, z, a, b, qkv_conv_w[l], a_log[l], dt_bias[l], gdn_norm_g[l])
        y_b = chunked_spatial_gating(jax.nn.gelu(u), jax.nn.gelu(v), sgu_ln_g[l], sgu_ln_b[l], sgu_w[l], sgu_b[l])
        merged = jax.nn.sigmoid(gate_a) * (y_a @ w_branch_a[l]) + jax.nn.sigmoid(gate_b) * (y_b @ w_branch_b[l])
        x = x + merged @ w_out[l]
        h = rms_norm(x, norm_ffn_g[l])
        up = depthwise_conv_centred(h @ w_up[l], ffn_conv_w[l]) + ffn_conv_b[l]
        c_gate, c_val = jnp.split(up, 2, axis=-1)
        x = x + (jax.nn.silu(c_gate) * c_val) @ w_down[l]
    return rms_norm(x, final_norm_g)
```

```python
import functools

import jax
import jax.numpy as jnp
from jax import lax
from jax.experimental import pallas as pl
from jax.experimental.pallas import tpu as pltpu

F32 = jnp.float32
BF16 = jnp.bfloat16

D_MODEL = 2048
GDN_HEADS = 8
HEAD_DIM = 128
GDN_WIDTH = GDN_HEADS * HEAD_DIM
QKV_CONV = 5
N_DIR = 2
SGU_GROUPS = 8
SGU_WIDTH = 1024
SGU_BLOCK = 128
D_FF = 5632
NORM_EPS = 1e-6

LANES = 128
CHUNK = 128
VMEM_LIMIT = 56 * 1024 * 1024


def _cparams(sem):
    return pltpu.CompilerParams(dimension_semantics=sem, vmem_limit_bytes=VMEM_LIMIT)


def _mm(a, b):
    return jnp.dot(a.astype(BF16), b.astype(BF16), preferred_element_type=F32)


def _mm_nt(a, b):
    return lax.dot_general(a.astype(BF16), b.astype(BF16), (((1,), (1,)), ((), ())),
                           preferred_element_type=F32)


def _rms(x, g):
    return x * lax.rsqrt(jnp.mean(x * x, axis=-1, keepdims=True) + NORM_EPS) * g


def _norm_matmul_kernel(x_ref, g_ref, w_ref, o_ref, h_ref):
    @pl.when(pl.program_id(1) == 0)
    def _():
        h_ref[...] = _rms(x_ref[...], g_ref[...]).astype(BF16)

    o_ref[...] = jnp.dot(h_ref[...], w_ref[...], preferred_element_type=F32).astype(o_ref.dtype)


def _norm_matmul(x, g, w, out_dtype, tm=512, tn=1024):
    m, d = x.shape
    n = w.shape[1]
    tn = min(tn, n)
    return pl.pallas_call(
        _norm_matmul_kernel,
        grid=(m // tm, n // tn),
        in_specs=[pl.BlockSpec((tm, d), lambda i, j: (i, 0)),
                  pl.BlockSpec((1, d), lambda i, j: (0, 0)),
                  pl.BlockSpec((d, tn), lambda i, j: (0, j))],
        out_specs=pl.BlockSpec((tm, tn), lambda i, j: (i, j)),
        out_shape=jax.ShapeDtypeStruct((m, n), out_dtype),
        scratch_shapes=[pltpu.VMEM((tm, d), BF16)],
        compiler_params=_cparams(("parallel", "arbitrary")),
        name="norm_matmul",
    )(x, g, w)


def _split3(x):
    hi = x.astype(BF16)
    r1 = x - hi.astype(F32)
    mid = r1.astype(BF16)
    lo = (r1 - mid.astype(F32)).astype(BF16)
    return hi, mid, lo


def _gate_prep_kernel(ab_ref, alog_ref, dtb_ref, gc_ref, gt_ref):
    seq = ab_ref.shape[1]
    lane = lax.broadcasted_iota(jnp.int32, (CHUNK, LANES), 1)
    ri = lax.broadcasted_iota(jnp.int32, (CHUNK, CHUNK), 0)
    ci = lax.broadcasted_iota(jnp.int32, (CHUNK, CHUNK), 1)
    lower = jnp.where(ci <= ri, 1.0, 0.0).astype(BF16)
    upper = jnp.where(ci >= ri, 1.0, 0.0).astype(BF16)
    neg_a = -jnp.exp(alog_ref[...])
    dtb = dtb_ref[...]
    for n in range(seq // CHUNK):
        ab = ab_ref[0, n * CHUNK:(n + 1) * CHUNK, :]
        z = ab + dtb
        softplus = jnp.maximum(z, 0.0) + jnp.log1p(jnp.exp(-jnp.abs(z)))
        g = neg_a * softplus
        beta = jax.nn.sigmoid(ab)
        pre = jnp.zeros((CHUNK, LANES), F32)
        suf = jnp.zeros((CHUNK, LANES), F32)
        for part in _split3(g):
            pre = pre + jnp.dot(lower, part, preferred_element_type=F32)
            suf = suf + jnp.dot(upper, part, preferred_element_type=F32)
        gsum = jnp.where(lane < GDN_HEADS, pre, suf)
        gc_ref[0, n * CHUNK:(n + 1) * CHUNK, :] = jnp.where(lane < N_DIR * GDN_HEADS, gsum, beta)
        gt_ref[0, n] = gsum.T[0:N_DIR * GDN_HEADS, :]


def _gate_prep(ab, alog, dtb):
    bsz, seq, _ = ab.shape
    nb = seq // CHUNK
    return pl.pallas_call(
        _gate_prep_kernel,
        grid=(bsz,),
        in_specs=[pl.BlockSpec((1, seq, LANES), lambda b: (b, 0, 0)),
                  pl.BlockSpec((1, LANES), lambda b: (0, 0)),
                  pl.BlockSpec((1, LANES), lambda b: (0, 0))],
        out_specs=[pl.BlockSpec((1, seq, LANES), lambda b: (b, 0, 0)),
                   pl.BlockSpec((1, nb, N_DIR * GDN_HEADS, CHUNK), lambda b: (b, 0, 0, 0))],
        out_shape=[jax.ShapeDtypeStruct((bsz, seq, LANES), F32),
                   jax.ShapeDtypeStruct((bsz, nb, N_DIR * GDN_HEADS, CHUNK), F32)],
        compiler_params=_cparams(("parallel",)),
        name="gate_prep",
    )(ab, alog, dtb)


def _conv_silu(x, w, seq):
    row = lax.broadcasted_iota(jnp.int32, x.shape, 0)
    half = QKV_CONV // 2
    acc = x * w[half:half + 1, :]
    for tap in range(QKV_CONV):
        d = tap - half
        if d == 0:
            continue
        shifted = pltpu.roll(x, (-d) % seq, axis=0)
        valid = jnp.logical_and(row + d >= 0, row + d < seq)
        acc = acc + jnp.where(valid, shifted, 0.0) * w[tap:tap + 1, :]
    return acc * jax.nn.sigmoid(acc)


def _unit_tri_inverse(a, ri, ci):
    eye = jnp.where(ri == ci, 1.0, 0.0)
    base = 16
    same = lambda sz: (ri >> (sz.bit_length() - 1)) == (ci >> (sz.bit_length() - 1))
    a0 = jnp.where(same(base), a, 0.0)
    t = eye - a0
    p = a0
    for _ in range(3):
        p = _mm(p, p)
        t = t + _mm(t, p)
    sz = base
    while sz < CHUNK:
        off = jnp.where(jnp.logical_and(same(2 * sz), jnp.logical_not(same(sz))), a, 0.0)
        t = t - _mm(_mm(t, off), t)
        sz *= 2
    return t


def _gdn_kernel(q_ref, k_ref, v_ref, z_ref, wq_ref, wk_ref, wv_ref, gc_ref, gt_ref, ng_ref, y_ref,
                qs, ks, vs, us, ws, qds, kdts, qks, decs, outs):
    seq = q_ref.shape[1]
    nb = seq // CHUNK
    h = pl.program_id(1)

    qc = _conv_silu(q_ref[0], wq_ref[...], seq)
    kc = _conv_silu(k_ref[0], wk_ref[...], seq)
    vs[...] = _conv_silu(v_ref[0], wv_ref[...], seq)
    qs[...] = qc * lax.rsqrt(jnp.sum(qc * qc, axis=-1, keepdims=True) + NORM_EPS) * (HEAD_DIM ** -0.5)
    ks[...] = kc * lax.rsqrt(jnp.sum(kc * kc, axis=-1, keepdims=True) + NORM_EPS)

    ri = lax.broadcasted_iota(jnp.int32, (CHUNK, CHUNK), 0)
    ci = lax.broadcasted_iota(jnp.int32, (CHUNK, CHUNK), 1)
    lane = lax.broadcasted_iota(jnp.int32, (CHUNK, LANES), 1)

    def prep(n, carry):
        rows = pl.ds(pl.multiple_of(n * CHUNK, CHUNK), CHUNK)
        q = qs[rows, :]
        k = ks[rows, :]
        v = vs[rows, :]
        gc = gc_ref[0, rows, :]
        for d in range(N_DIR):
            col = d * GDN_HEADS + h
            g_col = jnp.sum(jnp.where(lane == col, gc, 0.0), axis=1, keepdims=True)
            beta = jnp.sum(jnp.where(lane == N_DIR * GDN_HEADS + col, gc, 0.0), axis=1, keepdims=True)
            g_row = gt_ref[0, n, pl.ds(col, 1), :]
            incl = (ci <= ri) if d == 0 else (ci >= ri)
            strict = (ci < ri) if d == 0 else (ci > ri)
            decay = jnp.where(incl, jnp.exp(jnp.where(incl, g_col - g_row, 0.0)), 0.0)
            k_beta = k * beta
            a = jnp.where(strict, _mm_nt(k_beta, k) * decay, 0.0)
            qk = _mm_nt(q, k) * decay
            t = _unit_tri_inverse(a, ri, ci)
            eg = jnp.exp(g_col)
            vb = v * beta
            kbg = k_beta * eg
            t_off = jnp.where(ri == ci, 0.0, t)
            u = vb + _mm(t_off, vb)
            w = kbg + _mm(t_off, kbg)
            g_last = g_col[CHUNK - 1:CHUNK, :] if d == 0 else g_col[0:1, :]
            k_dec = k * jnp.exp(g_last - g_col)
            us[d, rows, :] = u
            ws[d, rows, :] = w.astype(BF16)
            qds[d, rows, :] = (q * eg).astype(BF16)
            kdts[d, n] = k_dec.T.astype(BF16)
            qks[d, rows, :] = qk.astype(BF16)
            decs[d, n] = jnp.broadcast_to(jnp.exp(g_last), (8, LANES))
        return carry

    lax.fori_loop(0, nb, prep, 0)

    def scan(s, states):
        new_states = []
        for d in range(N_DIR):
            n = s if d == 0 else nb - 1 - s
            rows = pl.ds(pl.multiple_of(n * CHUNK, CHUNK), CHUNK)
            st = states[d]
            st_b = st.astype(BF16)
            v_new = us[d, rows, :] - jnp.dot(ws[d, rows, :], st_b, preferred_element_type=F32)
            v_new_b = v_new.astype(BF16)
            o = (jnp.dot(qds[d, rows, :], st_b, preferred_element_type=F32)
                 + jnp.dot(qks[d, rows, :], v_new_b, preferred_element_type=F32))
            outs[d, rows, :] = o
            new_states.append(st * decs[d, n][0:1, :]
                              + jnp.dot(kdts[d, n], v_new_b, preferred_element_type=F32))
        return tuple(new_states)

    zero = jnp.zeros((HEAD_DIM, HEAD_DIM), F32)
    lax.fori_loop(0, nb, scan, (zero, zero))

    o = outs[0] + outs[1]
    zg = z_ref[0]
    y = _rms(o, ng_ref[...]) * (zg * jax.nn.sigmoid(zg))
    y_ref[0] = y.astype(y_ref.dtype)


def _gdn(qkvz, conv_w, gc, gt, norm_g):
    bsz, seq, _ = qkvz.shape
    nb = seq // CHUNK
    blk = lambda off: pl.BlockSpec((1, seq, HEAD_DIM), lambda b, h, off=off: (b, 0, off + h))
    wblk = lambda off: pl.BlockSpec((QKV_CONV, HEAD_DIM), lambda b, h, off=off: (0, off + h))
    return pl.pallas_call(
        _gdn_kernel,
        grid=(bsz, GDN_HEADS),
        in_specs=[blk(0), blk(GDN_HEADS), blk(2 * GDN_HEADS), blk(3 * GDN_HEADS),
                  wblk(0), wblk(GDN_HEADS), wblk(2 * GDN_HEADS),
                  pl.BlockSpec((1, seq, LANES), lambda b, h: (b, 0, 0)),
                  pl.BlockSpec((1, nb, N_DIR * GDN_HEADS, CHUNK), lambda b, h: (b, 0, 0, 0)),
                  pl.BlockSpec((1, HEAD_DIM), lambda b, h: (0, 0))],
        out_specs=pl.BlockSpec((1, seq, HEAD_DIM), lambda b, h: (b, 0, h)),
        out_shape=jax.ShapeDtypeStruct((bsz, seq, GDN_WIDTH), BF16),
        scratch_shapes=[pltpu.VMEM((seq, HEAD_DIM), F32),
                        pltpu.VMEM((seq, HEAD_DIM), F32),
                        pltpu.VMEM((seq, HEAD_DIM), F32),
                        pltpu.VMEM((N_DIR, seq, HEAD_DIM), F32),
                        pltpu.VMEM((N_DIR, seq, HEAD_DIM), BF16),
                        pltpu.VMEM((N_DIR, seq, HEAD_DIM), BF16),
                        pltpu.VMEM((N_DIR, nb, HEAD_DIM, CHUNK), BF16),
                        pltpu.VMEM((N_DIR, seq, CHUNK), BF16),
                        pltpu.VMEM((N_DIR, nb, 8, LANES), F32),
                        pltpu.VMEM((N_DIR, seq, HEAD_DIM), F32)],
        compiler_params=_cparams(("parallel", "parallel")),
        name="gdn",
    )(qkvz, qkvz, qkvz, qkvz, conv_w, conv_w, conv_w, gc, gt, norm_g)


def _sgu_kernel(u_ref, v_ref, lng_ref, lnb_ref, ws_ref, bs_ref, y_ref):
    rows = u_ref.shape[1]
    u = jax.nn.gelu(u_ref[0], approximate=True)
    v = jax.nn.gelu(v_ref[0], approximate=True)
    mu = jnp.mean(v, axis=-1, keepdims=True)
    vc = v - mu
    var = jnp.mean(vc * vc, axis=-1, keepdims=True)
    vn = (vc * lax.rsqrt(var + NORM_EPS) * lng_ref[...] + lnb_ref[...]).astype(BF16)
    for blk in range(rows // SGU_BLOCK):
        r = slice(blk * SGU_BLOCK, (blk + 1) * SGU_BLOCK)
        for g in range(SGU_GROUPS):
            c = slice(g * LANES, (g + 1) * LANES)
            s = jnp.dot(ws_ref[g], vn[r, c], preferred_element_type=F32) + bs_ref[g]
            y_ref[0, r, c] = (u[r, c] * s).astype(y_ref.dtype)


def _sgu(uv, ln_g, ln_b, w_s, b_s, ts=256):
    bsz, seq, _ = uv.shape
    return pl.pallas_call(
        _sgu_kernel,
        grid=(bsz, seq // ts),
        in_specs=[pl.BlockSpec((1, ts, SGU_WIDTH), lambda b, i: (b, i, 0)),
                  pl.BlockSpec((1, ts, SGU_WIDTH), lambda b, i: (b, i, 1)),
                  pl.BlockSpec((1, SGU_WIDTH), lambda b, i: (0, 0)),
                  pl.BlockSpec((1, SGU_WIDTH), lambda b, i: (0, 0)),
                  pl.BlockSpec((SGU_GROUPS, SGU_BLOCK, SGU_BLOCK), lambda b, i: (0, 0, 0)),
                  pl.BlockSpec((SGU_GROUPS, SGU_BLOCK, LANES), lambda b, i: (0, 0, 0))],
        out_specs=pl.BlockSpec((1, ts, SGU_WIDTH), lambda b, i: (b, i, 0)),
        out_shape=jax.ShapeDtypeStruct((bsz, seq, SGU_WIDTH), BF16),
        compiler_params=_cparams(("parallel", "parallel")),
        name="sgu",
    )(uv, uv, ln_g, ln_b, w_s, b_s)


def _merge_kernel(ya_ref, yb_ref, wa_ref, wb_ref, ga_ref, gb_ref, o_ref):
    pa = jnp.dot(ya_ref[...], wa_ref[...], preferred_element_type=F32)
    pb = jnp.dot(yb_ref[...], wb_ref[...], preferred_element_type=F32)
    ga = jax.nn.sigmoid(ga_ref[...].astype(F32))
    gb = jax.nn.sigmoid(gb_ref[...].astype(F32))
    o_ref[...] = (ga * pa + gb * pb).astype(o_ref.dtype)


def _merge(ya, yb, wa, wb, gates, tm=512, tn=1024):
    m, ka = ya.shape
    kb = yb.shape[1]
    n = wa.shape[1]
    nj = n // tn
    return pl.pallas_call(
        _merge_kernel,
        grid=(m // tm, nj),
        in_specs=[pl.BlockSpec((tm, ka), lambda i, j: (i, 0)),
                  pl.BlockSpec((tm, kb), lambda i, j: (i, 0)),
                  pl.BlockSpec((ka, tn), lambda i, j: (0, j)),
                  pl.BlockSpec((kb, tn), lambda i, j: (0, j)),
                  pl.BlockSpec((tm, tn), lambda i, j: (i, j)),
                  pl.BlockSpec((tm, tn), lambda i, j, nj=nj: (i, nj + j))],
        out_specs=pl.BlockSpec((tm, tn), lambda i, j: (i, j)),
        out_shape=jax.ShapeDtypeStruct((m, n), BF16),
        compiler_params=_cparams(("parallel", "parallel")),
        name="merge",
    )(ya, yb, wa, wb, gates, gates)


def _proj_residual_kernel(a_ref, w_ref, x_ref, o_ref):
    o_ref[...] = x_ref[...] + jnp.dot(a_ref[...], w_ref[...], preferred_element_type=F32)


def _proj_residual(a, w, x, tm=512, tn=1024):
    m, k = a.shape
    n = w.shape[1]
    return pl.pallas_call(
        _proj_residual_kernel,
        grid=(m // tm, n // tn),
        in_specs=[pl.BlockSpec((tm, k), lambda i, j: (i, 0)),
                  pl.BlockSpec((k, tn), lambda i, j: (0, j)),
                  pl.BlockSpec((tm, tn), lambda i, j: (i, j))],
        out_specs=pl.BlockSpec((tm, tn), lambda i, j: (i, j)),
        out_shape=jax.ShapeDtypeStruct((m, n), F32),
        compiler_params=_cparams(("parallel", "parallel")),
        name="proj_residual",
    )(a, w, x)


HALO = 8


def _ffn_kernel(x_ref, xp_ref, xn_ref, g_ref, wg_ref, wv_ref, cwg_ref, cwv_ref, cbg_ref, cbv_ref,
                wd_ref, o_ref, h_ref, acc_ref, *, tiles_per_seq):
    i = pl.program_id(0)
    f = pl.program_id(1)
    tm = x_ref.shape[0]

    @pl.when(f == 0)
    def _():
        g = g_ref[...]
        h_ref[0:tm, :] = _rms(x_ref[...], g).astype(BF16)
        t = i % tiles_per_seq
        hp = jnp.where(t == 0, 0.0, _rms(xp_ref[...], g))
        hn = jnp.where(t == tiles_per_seq - 1, 0.0, _rms(xn_ref[...], g))
        h_ref[tm:tm + 2 * HALO, :] = jnp.concatenate([hp, hn], axis=0).astype(BF16)
        acc_ref[...] = jnp.zeros_like(acc_ref)

    h = h_ref[...]
    row = lax.broadcasted_iota(jnp.int32, (tm, wg_ref.shape[1]), 0)

    def conv(w_ref, cw_ref, cb_ref):
        p = jnp.dot(h, w_ref[...], preferred_element_type=F32)
        main = p[0:tm, :]
        before = p[tm + HALO - 1:tm + HALO, :]
        after = p[tm + HALO:tm + HALO + 1, :]
        prev = jnp.where(row == 0, before, pltpu.roll(main, 1, axis=0))
        nxt = jnp.where(row == tm - 1, after, pltpu.roll(main, tm - 1, axis=0))
        cw = cw_ref[...]
        return prev * cw[0:1, :] + main * cw[1:2, :] + nxt * cw[2:3, :] + cb_ref[...]

    c_gate = conv(wg_ref, cwg_ref, cbg_ref)
    c_val = conv(wv_ref, cwv_ref, cbv_ref)
    act = (c_gate * jax.nn.sigmoid(c_gate) * c_val).astype(BF16)
    acc_ref[...] += jnp.dot(act, wd_ref[...], preferred_element_type=F32)

    @pl.when(f == pl.num_programs(1) - 1)
    def _():
        o_ref[...] = x_ref[...] + acc_ref[...]


def _ffn(x, g, w_up, conv_w, conv_b, w_down, seq, tm=512, tf=512):
    m, d = x.shape
    nf = D_FF // tf
    tiles_per_seq = seq // tm
    hb = tm // HALO
    last = m // HALO - 1
    return pl.pallas_call(
        functools.partial(_ffn_kernel, tiles_per_seq=tiles_per_seq),
        grid=(m // tm, nf),
        in_specs=[pl.BlockSpec((tm, d), lambda i, f: (i, 0)),
                  pl.BlockSpec((HALO, d), lambda i, f: (jnp.maximum(i * hb - 1, 0), 0)),
                  pl.BlockSpec((HALO, d), lambda i, f: (jnp.minimum((i + 1) * hb, last), 0)),
                  pl.BlockSpec((1, d), lambda i, f: (0, 0)),
                  pl.BlockSpec((d, tf), lambda i, f: (0, f)),
                  pl.BlockSpec((d, tf), lambda i, f, nf=nf: (0, nf + f)),
                  pl.BlockSpec((3, tf), lambda i, f: (0, f)),
                  pl.BlockSpec((3, tf), lambda i, f, nf=nf: (0, nf + f)),
                  pl.BlockSpec((1, tf), lambda i, f: (0, f)),
                  pl.BlockSpec((1, tf), lambda i, f, nf=nf: (0, nf + f)),
                  pl.BlockSpec((tf, d), lambda i, f: (f, 0))],
        out_specs=pl.BlockSpec((tm, d), lambda i, f: (i, 0)),
        out_shape=jax.ShapeDtypeStruct((m, d), F32),
        scratch_shapes=[pltpu.VMEM((tm + 2 * HALO, d), BF16),
                        pltpu.VMEM((tm, d), F32)],
        compiler_params=_cparams(("parallel", "arbitrary")),
        name="ffn",
    )(x, x, x, g, w_up, w_up, conv_w, conv_w, conv_b, conv_b, w_down)


def _rmsnorm_kernel(x_ref, g_ref, o_ref):
    o_ref[...] = _rms(x_ref[...], g_ref[...])


def _rmsnorm(x, g, tm=512):
    m, d = x.shape
    return pl.pallas_call(
        _rmsnorm_kernel,
        grid=(m // tm,),
        in_specs=[pl.BlockSpec((tm, d), lambda i: (i, 0)),
                  pl.BlockSpec((1, d), lambda i: (0, 0))],
        out_specs=pl.BlockSpec((tm, d), lambda i: (i, 0)),
        out_shape=jax.ShapeDtypeStruct((m, d), F32),
        compiler_params=_cparams(("parallel",)),
        name="final_norm",
    )(x, g)


def _pad_lanes(v):
    v = v.reshape(1, -1).astype(F32)
    return jnp.pad(v, ((0, 0), (0, LANES - v.shape[1])))


def kernel(x, norm_mix_g, w_in, qkv_conv_w, a_log, dt_bias, gdn_norm_g, w_branch_a, sgu_ln_g, sgu_ln_b,
           sgu_w, sgu_b, w_branch_b, w_out, norm_ffn_g, w_up, ffn_conv_w, ffn_conv_b, w_down, final_norm_g):
    bsz, seq, d = x.shape
    depth = w_in.shape[0]
    m = bsz * seq
    n_ab = 2 * N_DIR * GDN_HEADS
    o_qkvz = 0
    o_ab = 4 * GDN_WIDTH
    o_uv = o_ab + n_ab
    o_gates = o_uv + 2 * SGU_WIDTH
    xf = x.reshape(m, d)
    for l in range(depth):
        wl = w_in[l]
        w_qkvz = wl[:, o_qkvz:o_ab].astype(BF16)
        w_ab = jnp.pad(wl[:, o_ab:o_uv], ((0, 0), (0, LANES - n_ab))).astype(BF16)
        w_uv = wl[:, o_uv:o_gates].astype(BF16)
        w_gates = wl[:, o_gates:].astype(BF16)
        g_mix = norm_mix_g[l].reshape(1, d)

        qkvz = _norm_matmul(xf, g_mix, w_qkvz, F32)
        ab = _norm_matmul(xf, g_mix, w_ab, F32)
        uv = _norm_matmul(xf, g_mix, w_uv, F32)
        gates = _norm_matmul(xf, g_mix, w_gates, BF16)

        gc, gt = _gate_prep(ab.reshape(bsz, seq, LANES), _pad_lanes(a_log[l]), _pad_lanes(dt_bias[l]))
        y_a = _gdn(qkvz.reshape(bsz, seq, 4 * GDN_WIDTH), qkv_conv_w[l], gc, gt,
                   gdn_norm_g[l].reshape(1, HEAD_DIM))
        b_s = jnp.broadcast_to(sgu_b[l][:, :, None], (SGU_GROUPS, SGU_BLOCK, LANES))
        y_b = _sgu(uv.reshape(bsz, seq, 2 * SGU_WIDTH), sgu_ln_g[l].reshape(1, SGU_WIDTH),
                   sgu_ln_b[l].reshape(1, SGU_WIDTH), sgu_w[l].astype(BF16), b_s)

        merged = _merge(y_a.reshape(m, GDN_WIDTH), y_b.reshape(m, SGU_WIDTH),
                        w_branch_a[l].astype(BF16), w_branch_b[l].astype(BF16), gates)
        xf = _proj_residual(merged, w_out[l].astype(BF16), xf)

        xf = _ffn(xf, norm_ffn_g[l].reshape(1, d), w_up[l].astype(BF16), ffn_conv_w[l],
                  ffn_conv_b[l].reshape(1, 2 * D_FF), w_down[l].astype(BF16), seq)
    out = _rmsnorm(xf, final_norm_g.reshape(1, d))
    return out.reshape(bsz, seq, d)
```

```python
import functools

import jax
import jax.numpy as jnp
from jax import lax
from jax.experimental import pallas as pl
from jax.experimental.pallas import tpu as pltpu

F32 = jnp.float32
BF16 = jnp.bfloat16

D_MODEL = 2048
GDN_HEADS = 8
HEAD_DIM = 128
GDN_WIDTH = GDN_HEADS * HEAD_DIM
QKV_CONV = 5
N_DIR = 2
SGU_GROUPS = 8
SGU_WIDTH = 1024
SGU_BLOCK = 128
D_FF = 5632
NORM_EPS = 1e-6

LANES = 128
PREP_GROUP = 8
CHUNK = 128
VMEM_LIMIT = 56 * 1024 * 1024


def _cparams(sem):
    return pltpu.CompilerParams(dimension_semantics=sem, vmem_limit_bytes=VMEM_LIMIT)


def _mm(a, b):
    return jnp.dot(a.astype(BF16), b.astype(BF16), preferred_element_type=F32)


def _mm_nt(a, b):
    return lax.dot_general(a.astype(BF16), b.astype(BF16), (((1,), (1,)), ((), ())),
                           preferred_element_type=F32)


def _rms(x, g):
    return x * lax.rsqrt(jnp.mean(x * x, axis=-1, keepdims=True) + NORM_EPS) * g


def _norm_matmul_kernel(x_ref, g_ref, w_ref, o_ref, h_ref):
    @pl.when(pl.program_id(1) == 0)
    def _():
        h_ref[...] = _rms(x_ref[...], g_ref[...]).astype(BF16)

    o_ref[...] = jnp.dot(h_ref[...], w_ref[...], preferred_element_type=F32).astype(o_ref.dtype)


def _norm_matmul(x, g, w, out_dtype, tm=512, tn=1024):
    m, d = x.shape
    n = w.shape[1]
    tn = min(tn, n)
    return pl.pallas_call(
        _norm_matmul_kernel,
        grid=(m // tm, n // tn),
        in_specs=[pl.BlockSpec((tm, d), lambda i, j: (i, 0)),
                  pl.BlockSpec((1, d), lambda i, j: (0, 0)),
                  pl.BlockSpec((d, tn), lambda i, j: (0, j))],
        out_specs=pl.BlockSpec((tm, tn), lambda i, j: (i, j)),
        out_shape=jax.ShapeDtypeStruct((m, n), out_dtype),
        scratch_shapes=[pltpu.VMEM((tm, d), BF16)],
        compiler_params=_cparams(("parallel", "arbitrary")),
        name="norm_matmul",
    )(x, g, w)


def _split3(x):
    hi = x.astype(BF16)
    r1 = x - hi.astype(F32)
    mid = r1.astype(BF16)
    lo = (r1 - mid.astype(F32)).astype(BF16)
    return hi, mid, lo


def _gate_prep_kernel(ab_ref, alog_ref, dtb_ref, gc_ref, gt_ref):
    seq = ab_ref.shape[1]
    lane = lax.broadcasted_iota(jnp.int32, (CHUNK, LANES), 1)
    ri = lax.broadcasted_iota(jnp.int32, (CHUNK, CHUNK), 0)
    ci = lax.broadcasted_iota(jnp.int32, (CHUNK, CHUNK), 1)
    lower = jnp.where(ci <= ri, 1.0, 0.0).astype(BF16)
    upper = jnp.where(ci >= ri, 1.0, 0.0).astype(BF16)
    neg_a = -jnp.exp(alog_ref[...])
    dtb = dtb_ref[...]
    for n in range(seq // CHUNK):
        ab = ab_ref[0, n * CHUNK:(n + 1) * CHUNK, :]
        z = ab + dtb
        softplus = jnp.maximum(z, 0.0) + jnp.log1p(jnp.exp(-jnp.abs(z)))
        g = neg_a * softplus
        beta = jax.nn.sigmoid(ab)
        pre = jnp.zeros((CHUNK, LANES), F32)
        suf = jnp.zeros((CHUNK, LANES), F32)
        for part in _split3(g):
            pre = pre + jnp.dot(lower, part, preferred_element_type=F32)
            suf = suf + jnp.dot(upper, part, preferred_element_type=F32)
        gsum = jnp.where(lane < GDN_HEADS, pre, suf)
        gc_ref[0, n * CHUNK:(n + 1) * CHUNK, :] = jnp.where(lane < N_DIR * GDN_HEADS, gsum, beta)
        gt_ref[0, n] = gsum.T[0:N_DIR * GDN_HEADS, :]


def _gate_prep(ab, alog, dtb):
    bsz, seq, _ = ab.shape
    nb = seq // CHUNK
    return pl.pallas_call(
        _gate_prep_kernel,
        grid=(bsz,),
        in_specs=[pl.BlockSpec((1, seq, LANES), lambda b: (b, 0, 0)),
                  pl.BlockSpec((1, LANES), lambda b: (0, 0)),
                  pl.BlockSpec((1, LANES), lambda b: (0, 0))],
        out_specs=[pl.BlockSpec((1, seq, LANES), lambda b: (b, 0, 0)),
                   pl.BlockSpec((1, nb, N_DIR * GDN_HEADS, CHUNK), lambda b: (b, 0, 0, 0))],
        out_shape=[jax.ShapeDtypeStruct((bsz, seq, LANES), F32),
                   jax.ShapeDtypeStruct((bsz, nb, N_DIR * GDN_HEADS, CHUNK), F32)],
        compiler_params=_cparams(("parallel",)),
        name="gate_prep",
    )(ab, alog, dtb)


def _conv_silu(x, w, seq):
    row = lax.broadcasted_iota(jnp.int32, x.shape, 0)
    half = QKV_CONV // 2
    acc = x * w[half:half + 1, :]
    for tap in range(QKV_CONV):
        d = tap - half
        if d == 0:
            continue
        shifted = pltpu.roll(x, (-d) % seq, axis=0)
        valid = jnp.logical_and(row + d >= 0, row + d < seq)
        acc = acc + jnp.where(valid, shifted, 0.0) * w[tap:tap + 1, :]
    return acc * jax.nn.sigmoid(acc)


def _unit_tri_inverse(mats, ri, ci):
    eye = jnp.where(ri == ci, 1.0, 0.0)
    base = 16
    same = lambda sz: (ri >> (sz.bit_length() - 1)) == (ci >> (sz.bit_length() - 1))
    ps = [jnp.where(same(base), a, 0.0) for a in mats]
    ts = [eye - p for p in ps]
    for _ in range(3):
        ps = [_mm(p, p) for p in ps]
        ts = [t + _mm(t, p) for t, p in zip(ts, ps)]
    sz = base
    while sz < CHUNK:
        sel = jnp.logical_and(same(2 * sz), jnp.logical_not(same(sz)))
        xs = [_mm(t, jnp.where(sel, a, 0.0)) for t, a in zip(ts, mats)]
        ts = [t - _mm(x, t) for t, x in zip(ts, xs)]
        sz *= 2
    return ts


def _gdn_kernel(q_ref, k_ref, v_ref, z_ref, wq_ref, wk_ref, wv_ref, gc_ref, gt_ref, ng_ref, y_ref,
                qs, ks, vs, us, ws, qds, kdts, qks, decs, outs):
    seq = q_ref.shape[1]
    nb = seq // CHUNK
    h = pl.program_id(1)

    qc = _conv_silu(q_ref[0], wq_ref[...], seq)
    kc = _conv_silu(k_ref[0], wk_ref[...], seq)
    vs[...] = _conv_silu(v_ref[0], wv_ref[...], seq)
    qs[...] = qc * lax.rsqrt(jnp.sum(qc * qc, axis=-1, keepdims=True) + NORM_EPS) * (HEAD_DIM ** -0.5)
    ks[...] = kc * lax.rsqrt(jnp.sum(kc * kc, axis=-1, keepdims=True) + NORM_EPS)

    ri = lax.broadcasted_iota(jnp.int32, (CHUNK, CHUNK), 0)
    ci = lax.broadcasted_iota(jnp.int32, (CHUNK, CHUNK), 1)
    lane = lax.broadcasted_iota(jnp.int32, (CHUNK, LANES), 1)

    def prep(i, carry):
        chains = []
        for j in range(PREP_GROUP):
            n = i * PREP_GROUP + j
            rows = pl.ds(pl.multiple_of(n * CHUNK, CHUNK), CHUNK)
            q = qs[rows, :]
            k = ks[rows, :]
            v = vs[rows, :]
            gc = gc_ref[0, rows, :]
            for d in range(N_DIR):
                col = d * GDN_HEADS + h
                g_col = jnp.sum(jnp.where(lane == col, gc, 0.0), axis=1, keepdims=True)
                beta = jnp.sum(jnp.where(lane == N_DIR * GDN_HEADS + col, gc, 0.0), axis=1, keepdims=True)
                g_row = gt_ref[0, n, pl.ds(col, 1), :]
                incl = (ci <= ri) if d == 0 else (ci >= ri)
                decay = jnp.where(incl, jnp.exp(jnp.where(incl, g_col - g_row, 0.0)), 0.0)
                chains.append(dict(n=n, rows=rows, d=d, q=q, k=k, v=v, g_col=g_col, beta=beta,
                                   decay=decay, k_beta=k * beta))
        for c in chains:
            strict = (ci < ri) if c["d"] == 0 else (ci > ri)
            c["a"] = jnp.where(strict, _mm_nt(c["k_beta"], c["k"]) * c["decay"], 0.0)
        invs = _unit_tri_inverse([c["a"] for c in chains], ri, ci)
        for c, t in zip(chains, invs):
            d, n, rows, g_col, k, q = c["d"], c["n"], c["rows"], c["g_col"], c["k"], c["q"]
            eg = jnp.exp(g_col)
            vb = c["v"] * c["beta"]
            kbg = c["k_beta"] * eg
            t_off = jnp.where(ri == ci, 0.0, t)
            us[d, rows, :] = vb + _mm(t_off, vb)
            ws[d, rows, :] = (kbg + _mm(t_off, kbg)).astype(BF16)
            g_last = g_col[CHUNK - 1:CHUNK, :] if d == 0 else g_col[0:1, :]
            qds[d, rows, :] = (q * eg).astype(BF16)
            kdts[d, n] = (k * jnp.exp(g_last - g_col)).T.astype(BF16)
            qks[d, rows, :] = (_mm_nt(q, k) * c["decay"]).astype(BF16)
            decs[d, n] = jnp.broadcast_to(jnp.exp(g_last), (8, LANES))
        return carry

    lax.fori_loop(0, nb // PREP_GROUP, prep, 0)

    def scan(s, states):
        dirs = range(N_DIR)
        ns = [s, nb - 1 - s]
        rows = [pl.ds(pl.multiple_of(n * CHUNK, CHUNK), CHUNK) for n in ns]
        st_b = [states[d].astype(BF16) for d in dirs]
        ws_st = [jnp.dot(ws[d, rows[d], :], st_b[d], preferred_element_type=F32) for d in dirs]
        v_new = [(us[d, rows[d], :] - ws_st[d]).astype(BF16) for d in dirs]
        upd = [jnp.dot(kdts[d, ns[d]], v_new[d], preferred_element_type=F32) for d in dirs]
        new_states = tuple(states[d] * decs[d, ns[d]][0:1, :] + upd[d] for d in dirs)
        for d in dirs:
            outs[d, rows[d], :] = (jnp.dot(qds[d, rows[d], :], st_b[d], preferred_element_type=F32)
                                   + jnp.dot(qks[d, rows[d], :], v_new[d], preferred_element_type=F32))
        return new_states

    zero = jnp.zeros((HEAD_DIM, HEAD_DIM), F32)
    lax.fori_loop(0, nb, scan, (zero, zero))

    o = outs[0] + outs[1]
    zg = z_ref[0]
    y = _rms(o, ng_ref[...]) * (zg * jax.nn.sigmoid(zg))
    y_ref[0] = y.astype(y_ref.dtype)


def _gdn(qkvz, conv_w, gc, gt, norm_g):
    bsz, seq, _ = qkvz.shape
    nb = seq // CHUNK
    blk = lambda off: pl.BlockSpec((1, seq, HEAD_DIM), lambda b, h, off=off: (b, 0, off + h))
    wblk = lambda off: pl.BlockSpec((QKV_CONV, HEAD_DIM), lambda b, h, off=off: (0, off + h))
    return pl.pallas_call(
        _gdn_kernel,
        grid=(bsz, GDN_HEADS),
        in_specs=[blk(0), blk(GDN_HEADS), blk(2 * GDN_HEADS), blk(3 * GDN_HEADS),
                  wblk(0), wblk(GDN_HEADS), wblk(2 * GDN_HEADS),
                  pl.BlockSpec((1, seq, LANES), lambda b, h: (b, 0, 0)),
                  pl.BlockSpec((1, nb, N_DIR * GDN_HEADS, CHUNK), lambda b, h: (b, 0, 0, 0)),
                  pl.BlockSpec((1, HEAD_DIM), lambda b, h: (0, 0))],
        out_specs=pl.BlockSpec((1, seq, HEAD_DIM), lambda b, h: (b, 0, h)),
        out_shape=jax.ShapeDtypeStruct((bsz, seq, GDN_WIDTH), BF16),
        scratch_shapes=[pltpu.VMEM((seq, HEAD_DIM), F32),
                        pltpu.VMEM((seq, HEAD_DIM), F32),
                        pltpu.VMEM((seq, HEAD_DIM), F32),
                        pltpu.VMEM((N_DIR, seq, HEAD_DIM), F32),
                        pltpu.VMEM((N_DIR, seq, HEAD_DIM), BF16),
                        pltpu.VMEM((N_DIR, seq, HEAD_DIM), BF16),
                        pltpu.VMEM((N_DIR, nb, HEAD_DIM, CHUNK), BF16),
                        pltpu.VMEM((N_DIR, seq, CHUNK), BF16),
                        pltpu.VMEM((N_DIR, nb, 8, LANES), F32),
                        pltpu.VMEM((N_DIR, seq, HEAD_DIM), F32)],
        compiler_params=_cparams(("parallel", "parallel")),
        name="gdn",
    )(qkvz, qkvz, qkvz, qkvz, conv_w, conv_w, conv_w, gc, gt, norm_g)


def _sgu_kernel(u_ref, v_ref, lng_ref, lnb_ref, ws_ref, bs_ref, y_ref):
    rows = u_ref.shape[1]
    u = jax.nn.gelu(u_ref[0], approximate=True)
    v = jax.nn.gelu(v_ref[0], approximate=True)
    mu = jnp.mean(v, axis=-1, keepdims=True)
    vc = v - mu
    var = jnp.mean(vc * vc, axis=-1, keepdims=True)
    vn = (vc * lax.rsqrt(var + NORM_EPS) * lng_ref[...] + lnb_ref[...]).astype(BF16)
    for blk in range(rows // SGU_BLOCK):
        r = slice(blk * SGU_BLOCK, (blk + 1) * SGU_BLOCK)
        for g in range(SGU_GROUPS):
            c = slice(g * LANES, (g + 1) * LANES)
            s = jnp.dot(ws_ref[g], vn[r, c], preferred_element_type=F32) + bs_ref[g]
            y_ref[0, r, c] = (u[r, c] * s).astype(y_ref.dtype)


def _sgu(uv, ln_g, ln_b, w_s, b_s, ts=256):
    bsz, seq, _ = uv.shape
    return pl.pallas_call(
        _sgu_kernel,
        grid=(bsz, seq // ts),
        in_specs=[pl.BlockSpec((1, ts, SGU_WIDTH), lambda b, i: (b, i, 0)),
                  pl.BlockSpec((1, ts, SGU_WIDTH), lambda b, i: (b, i, 1)),
                  pl.BlockSpec((1, SGU_WIDTH), lambda b, i: (0, 0)),
                  pl.BlockSpec((1, SGU_WIDTH), lambda b, i: (0, 0)),
                  pl.BlockSpec((SGU_GROUPS, SGU_BLOCK, SGU_BLOCK), lambda b, i: (0, 0, 0)),
                  pl.BlockSpec((SGU_GROUPS, SGU_BLOCK, LANES), lambda b, i: (0, 0, 0))],
        out_specs=pl.BlockSpec((1, ts, SGU_WIDTH), lambda b, i: (b, i, 0)),
        out_shape=jax.ShapeDtypeStruct((bsz, seq, SGU_WIDTH), BF16),
        compiler_params=_cparams(("parallel", "parallel")),
        name="sgu",
    )(uv, uv, ln_g, ln_b, w_s, b_s)


def _merge_kernel(ya_ref, yb_ref, wa_ref, wb_ref, ga_ref, gb_ref, o_ref):
    pa = jnp.dot(ya_ref[...], wa_ref[...], preferred_element_type=F32)
    pb = jnp.dot(yb_ref[...], wb_ref[...], preferred_element_type=F32)
    ga = jax.nn.sigmoid(ga_ref[...].astype(F32))
    gb = jax.nn.sigmoid(gb_ref[...].astype(F32))
    o_ref[...] = (ga * pa + gb * pb).astype(o_ref.dtype)


def _merge(ya, yb, wa, wb, gates, tm=512, tn=1024):
    m, ka = ya.shape
    kb = yb.shape[1]
    n = wa.shape[1]
    nj = n // tn
    return pl.pallas_call(
        _merge_kernel,
        grid=(m // tm, nj),
        in_specs=[pl.BlockSpec((tm, ka), lambda i, j: (i, 0)),
                  pl.BlockSpec((tm, kb), lambda i, j: (i, 0)),
                  pl.BlockSpec((ka, tn), lambda i, j: (0, j)),
                  pl.BlockSpec((kb, tn), lambda i, j: (0, j)),
                  pl.BlockSpec((tm, tn), lambda i, j: (i, j)),
                  pl.BlockSpec((tm, tn), lambda i, j, nj=nj: (i, nj + j))],
        out_specs=pl.BlockSpec((tm, tn), lambda i, j: (i, j)),
        out_shape=jax.ShapeDtypeStruct((m, n), BF16),
        compiler_params=_cparams(("parallel", "parallel")),
        name="merge",
    )(ya, yb, wa, wb, gates, gates)


def _proj_residual_kernel(a_ref, w_ref, x_ref, o_ref):
    o_ref[...] = x_ref[...] + jnp.dot(a_ref[...], w_ref[...], preferred_element_type=F32)


def _proj_residual(a, w, x, tm=512, tn=1024):
    m, k = a.shape
    n = w.shape[1]
    return pl.pallas_call(
        _proj_residual_kernel,
        grid=(m // tm, n // tn),
        in_specs=[pl.BlockSpec((tm, k), lambda i, j: (i, 0)),
                  pl.BlockSpec((k, tn), lambda i, j: (0, j)),
                  pl.BlockSpec((tm, tn), lambda i, j: (i, j))],
        out_specs=pl.BlockSpec((tm, tn), lambda i, j: (i, j)),
        out_shape=jax.ShapeDtypeStruct((m, n), F32),
        compiler_params=_cparams(("parallel", "parallel")),
        name="proj_residual",
    )(a, w, x)


HALO = 8


def _ffn_kernel(x_ref, xp_ref, xn_ref, g_ref, wg_ref, wv_ref, cwg_ref, cwv_ref, cbg_ref, cbv_ref,
                wd_ref, o_ref, h_ref, acc_ref, *, tiles_per_seq):
    i = pl.program_id(0)
    f = pl.program_id(1)
    tm = x_ref.shape[0]

    @pl.when(f == 0)
    def _():
        g = g_ref[...]
        h_ref[0:tm, :] = _rms(x_ref[...], g).astype(BF16)
        t = i % tiles_per_seq
        hp = jnp.where(t == 0, 0.0, _rms(xp_ref[...], g))
        hn = jnp.where(t == tiles_per_seq - 1, 0.0, _rms(xn_ref[...], g))
        h_ref[tm:tm + 2 * HALO, :] = jnp.concatenate([hp, hn], axis=0).astype(BF16)
        acc_ref[...] = jnp.zeros_like(acc_ref)

    h = h_ref[...]
    row = lax.broadcasted_iota(jnp.int32, (tm, wg_ref.shape[1]), 0)

    def conv(w_ref, cw_ref, cb_ref):
        p = jnp.dot(h, w_ref[...], preferred_element_type=F32)
        main = p[0:tm, :]
        before = p[tm + HALO - 1:tm + HALO, :]
        after = p[tm + HALO:tm + HALO + 1, :]
        prev = jnp.where(row == 0, before, pltpu.roll(main, 1, axis=0))
        nxt = jnp.where(row == tm - 1, after, pltpu.roll(main, tm - 1, axis=0))
        cw = cw_ref[...]
        return prev * cw[0:1, :] + main * cw[1:2, :] + nxt * cw[2:3, :] + cb_ref[...]

    c_gate = conv(wg_ref, cwg_ref, cbg_ref)
    c_val = conv(wv_ref, cwv_ref, cbv_ref)
    act = (c_gate * jax.nn.sigmoid(c_gate) * c_val).astype(BF16)
    acc_ref[...] += jnp.dot(act, wd_ref[...], preferred_element_type=F32)

    @pl.when(f == pl.num_programs(1) - 1)
    def _():
        o_ref[...] = x_ref[...] + acc_ref[...]


def _ffn(x, g, w_up, conv_w, conv_b, w_down, seq, tm=512, tf=512):
    m, d = x.shape
    nf = D_FF // tf
    tiles_per_seq = seq // tm
    hb = tm // HALO
    last = m // HALO - 1
    return pl.pallas_call(
        functools.partial(_ffn_kernel, tiles_per_seq=tiles_per_seq),
        grid=(m // tm, nf),
        in_specs=[pl.BlockSpec((tm, d), lambda i, f: (i, 0)),
                  pl.BlockSpec((HALO, d), lambda i, f: (jnp.maximum(i * hb - 1, 0), 0)),
                  pl.BlockSpec((HALO, d), lambda i, f: (jnp.minimum((i + 1) * hb, last), 0)),
                  pl.BlockSpec((1, d), lambda i, f: (0, 0)),
                  pl.BlockSpec((d, tf), lambda i, f: (0, f)),
                  pl.BlockSpec((d, tf), lambda i, f, nf=nf: (0, nf + f)),
                  pl.BlockSpec((3, tf), lambda i, f: (0, f)),
                  pl.BlockSpec((3, tf), lambda i, f, nf=nf: (0, nf + f)),
                  pl.BlockSpec((1, tf), lambda i, f: (0, f)),
                  pl.BlockSpec((1, tf), lambda i, f, nf=nf: (0, nf + f)),
                  pl.BlockSpec((tf, d), lambda i, f: (f, 0))],
        out_specs=pl.BlockSpec((tm, d), lambda i, f: (i, 0)),
        out_shape=jax.ShapeDtypeStruct((m, d), F32),
        scratch_shapes=[pltpu.VMEM((tm + 2 * HALO, d), BF16),
                        pltpu.VMEM((tm, d), F32)],
        compiler_params=_cparams(("parallel", "arbitrary")),
        name="ffn",
    )(x, x, x, g, w_up, w_up, conv_w, conv_w, conv_b, conv_b, w_down)


def _rmsnorm_kernel(x_ref, g_ref, o_ref):
    o_ref[...] = _rms(x_ref[...], g_ref[...])


def _rmsnorm(x, g, tm=512):
    m, d = x.shape
    return pl.pallas_call(
        _rmsnorm_kernel,
        grid=(m // tm,),
        in_specs=[pl.BlockSpec((tm, d), lambda i: (i, 0)),
                  pl.BlockSpec((1, d), lambda i: (0, 0))],
        out_specs=pl.BlockSpec((tm, d), lambda i: (i, 0)),
        out_shape=jax.ShapeDtypeStruct((m, d), F32),
        compiler_params=_cparams(("parallel",)),
        name="final_norm",
    )(x, g)


def _pad_lanes(v):
    v = v.reshape(1, -1).astype(F32)
    return jnp.pad(v, ((0, 0), (0, LANES - v.shape[1])))


def kernel(x, norm_mix_g, w_in, qkv_conv_w, a_log, dt_bias, gdn_norm_g, w_branch_a, sgu_ln_g, sgu_ln_b,
           sgu_w, sgu_b, w_branch_b, w_out, norm_ffn_g, w_up, ffn_conv_w, ffn_conv_b, w_down, final_norm_g):
    bsz, seq, d = x.shape
    depth = w_in.shape[0]
    m = bsz * seq
    n_ab = 2 * N_DIR * GDN_HEADS
    o_qkvz = 0
    o_ab = 4 * GDN_WIDTH
    o_uv = o_ab + n_ab
    o_gates = o_uv + 2 * SGU_WIDTH
    xf = x.reshape(m, d)
    for l in range(depth):
        wl = w_in[l]
        w_qkvz = wl[:, o_qkvz:o_ab].astype(BF16)
        w_ab = jnp.pad(wl[:, o_ab:o_uv], ((0, 0), (0, LANES - n_ab))).astype(BF16)
        w_uv = wl[:, o_uv:o_gates].astype(BF16)
        w_gates = wl[:, o_gates:].astype(BF16)
        g_mix = norm_mix_g[l].reshape(1, d)

        qkvz = _norm_matmul(xf, g_mix, w_qkvz, F32)
        ab = _norm_matmul(xf, g_mix, w_ab, F32)
        uv = _norm_matmul(xf, g_mix, w_uv, F32)
        gates = _norm_matmul(xf, g_mix, w_gates, BF16)

        gc, gt = _gate_prep(ab.reshape(bsz, seq, LANES), _pad_lanes(a_log[l]), _pad_lanes(dt_bias[l]))
        y_a = _gdn(qkvz.reshape(bsz, seq, 4 * GDN_WIDTH), qkv_conv_w[l], gc, gt,
                   gdn_norm_g[l].reshape(1, HEAD_DIM))
        b_s = jnp.broadcast_to(sgu_b[l][:, :, None], (SGU_GROUPS, SGU_BLOCK, LANES))
        y_b = _sgu(uv.reshape(bsz, seq, 2 * SGU_WIDTH), sgu_ln_g[l].reshape(1, SGU_WIDTH),
                   sgu_ln_b[l].reshape(1, SGU_WIDTH), sgu_w[l].astype(BF16), b_s)

        merged = _merge(y_a.reshape(m, GDN_WIDTH), y_b.reshape(m, SGU_WIDTH),
                        w_branch_a[l].astype(BF16), w_branch_b[l].astype(BF16), gates)
        xf = _proj_residual(merged, w_out[l].astype(BF16), xf)

        xf = _ffn(xf, norm_ffn_g[l].reshape(1, d), w_up[l].astype(BF16), ffn_conv_w[l],
                  ffn_conv_b[l].reshape(1, 2 * D_FF), w_down[l].astype(BF16), seq)
    out = _rmsnorm(xf, final_norm_g.reshape(1, d))
    return out.reshape(bsz, seq, d)
```

```python
import functools

import jax
import jax.numpy as jnp
from jax import lax
from jax.experimental import pallas as pl
from jax.experimental.pallas import tpu as pltpu

F32 = jnp.float32
BF16 = jnp.bfloat16

D_MODEL = 2048
GDN_HEADS = 8
HEAD_DIM = 128
GDN_WIDTH = GDN_HEADS * HEAD_DIM
QKV_CONV = 5
N_DIR = 2
SGU_GROUPS = 8
SGU_WIDTH = 1024
SGU_BLOCK = 128
D_FF = 5632
NORM_EPS = 1e-6
N_AB = 2 * N_DIR * GDN_HEADS
COL_AB = 4 * GDN_WIDTH
COL_UV = COL_AB + N_AB

LANES = 128
SUBLANES = 8
GDN_HEADS_PER_STEP = 2
PREP_GROUP = 8
CHUNK = 128
VMEM_LIMIT = 56 * 1024 * 1024
TM = 1024
TN = 1024


def _cparams(sem):
    return pltpu.CompilerParams(dimension_semantics=sem, vmem_limit_bytes=VMEM_LIMIT)


def _mm(a, b):
    return jnp.dot(a.astype(BF16), b.astype(BF16), preferred_element_type=F32)


def _mm_nt(a, b):
    return lax.dot_general(a.astype(BF16), b.astype(BF16), (((1,), (1,)), ((), ())),
                           preferred_element_type=F32)


def _rms(x, g):
    return x * lax.rsqrt(jnp.mean(x * x, axis=-1, keepdims=True) + NORM_EPS) * g


def _rmsnorm_kernel(x_ref, g_ref, o_ref):
    o_ref[...] = _rms(x_ref[...], g_ref[...]).astype(o_ref.dtype)


def _rmsnorm(x, g, out_dtype, tm=512):
    m, d = x.shape
    return pl.pallas_call(
        _rmsnorm_kernel,
        grid=(m // tm,),
        in_specs=[pl.BlockSpec((tm, d), lambda i: (i, 0)),
                  pl.BlockSpec((1, d), lambda i: (0, 0))],
        out_specs=pl.BlockSpec((tm, d), lambda i: (i, 0)),
        out_shape=jax.ShapeDtypeStruct((m, d), out_dtype),
        compiler_params=_cparams(("parallel",)),
        name="rmsnorm",
    )(x, g)


def _proj_kernel(a_ref, w_ref, o_ref, wb_ref):
    @pl.when(pl.program_id(1) == 0)
    def _():
        wb_ref[...] = w_ref[...].astype(BF16)

    o_ref[...] = jnp.dot(a_ref[...], wb_ref[...], preferred_element_type=F32).astype(o_ref.dtype)


def _proj(a, w_all, layer, n_cols, out_dtype):
    m, k = a.shape
    return pl.pallas_call(
        _proj_kernel,
        grid=(n_cols // TN, m // TM),
        in_specs=[pl.BlockSpec((TM, k), lambda j, i: (i, 0)),
                  pl.BlockSpec((None, k, TN), lambda j, i: (layer, 0, j))],
        out_specs=pl.BlockSpec((TM, TN), lambda j, i: (i, j)),
        out_shape=jax.ShapeDtypeStruct((m, n_cols), out_dtype),
        scratch_shapes=[pltpu.VMEM((k, TN), BF16)],
        compiler_params=_cparams(("parallel", "arbitrary")),
        name="proj",
    )(a, w_all)


def _proj_shift_kernel(a_ref, w_ref, wx_ref, o_ref, wb_ref, *, shift):
    @pl.when(pl.program_id(1) == 0)
    def _():
        tn = wb_ref.shape[1]
        wide = jnp.concatenate([w_ref[...], wx_ref[...]], axis=1)
        wb_ref[...] = wide[:, shift:shift + tn].astype(BF16)

    o_ref[...] = jnp.dot(a_ref[...], wb_ref[...], preferred_element_type=F32).astype(o_ref.dtype)


def _proj_shifted(a, w_all, layer, col0, n_cols, out_dtype):
    m, k = a.shape
    base = (col0 // TN) * TN
    shift = col0 - base
    assert 0 < shift < LANES and base % TN == 0
    xblk = TN // LANES
    return pl.pallas_call(
        functools.partial(_proj_shift_kernel, shift=shift),
        grid=(n_cols // TN, m // TM),
        in_specs=[pl.BlockSpec((TM, k), lambda j, i: (i, 0)),
                  pl.BlockSpec((None, k, TN), lambda j, i: (layer, 0, base // TN + j)),
                  pl.BlockSpec((None, k, LANES), lambda j, i: (layer, 0, (base // TN + j + 1) * xblk))],
        out_specs=pl.BlockSpec((TM, TN), lambda j, i: (i, j)),
        out_shape=jax.ShapeDtypeStruct((m, n_cols), out_dtype),
        scratch_shapes=[pltpu.VMEM((k, TN), BF16)],
        compiler_params=_cparams(("parallel", "arbitrary")),
        name="proj_shifted",
    )(a, w_all, w_all)


def _small_proj_kernel(a_ref, w_ref, o_ref):
    o_ref[...] = jnp.dot(a_ref[...], w_ref[...], preferred_element_type=F32)


def _small_proj(a, w):
    m, k = a.shape
    n = w.shape[1]
    return pl.pallas_call(
        _small_proj_kernel,
        grid=(m // TM,),
        in_specs=[pl.BlockSpec((TM, k), lambda i: (i, 0)),
                  pl.BlockSpec((k, n), lambda i: (0, 0))],
        out_specs=pl.BlockSpec((TM, n), lambda i: (i, 0)),
        out_shape=jax.ShapeDtypeStruct((m, n), F32),
        compiler_params=_cparams(("parallel",)),
        name="small_proj",
    )(a, w)


def _split3(x):
    hi = x.astype(BF16)
    r1 = x - hi.astype(F32)
    mid = r1.astype(BF16)
    lo = (r1 - mid.astype(F32)).astype(BF16)
    return hi, mid, lo


def _gate_prep_kernel(ab_ref, alog_ref, dtb_ref, gc_ref, gt_ref):
    seq = ab_ref.shape[1]
    lane = lax.broadcasted_iota(jnp.int32, (CHUNK, LANES), 1)
    ri = lax.broadcasted_iota(jnp.int32, (CHUNK, CHUNK), 0)
    ci = lax.broadcasted_iota(jnp.int32, (CHUNK, CHUNK), 1)
    lower = jnp.where(ci <= ri, 1.0, 0.0).astype(BF16)
    upper = jnp.where(ci >= ri, 1.0, 0.0).astype(BF16)
    neg_a = -jnp.exp(alog_ref[...])
    dtb = dtb_ref[...]
    for n in range(seq // CHUNK):
        ab = ab_ref[0, n * CHUNK:(n + 1) * CHUNK, :]
        z = ab + dtb
        softplus = jnp.maximum(z, 0.0) + jnp.log1p(jnp.exp(-jnp.abs(z)))
        g = neg_a * softplus
        beta = jax.nn.sigmoid(ab)
        pre = jnp.zeros((CHUNK, LANES), F32)
        suf = jnp.zeros((CHUNK, LANES), F32)
        for part in _split3(g):
            pre = pre + jnp.dot(lower, part, preferred_element_type=F32)
            suf = suf + jnp.dot(upper, part, preferred_element_type=F32)
        gsum = jnp.where(lane < GDN_HEADS, pre, suf)
        gc_ref[0, n * CHUNK:(n + 1) * CHUNK, :] = jnp.where(lane < N_DIR * GDN_HEADS, gsum, beta)
        gt_ref[0, n] = gsum.T[0:N_DIR * GDN_HEADS, :]


def _gate_prep(ab, alog, dtb):
    bsz, seq, _ = ab.shape
    nb = seq // CHUNK
    return pl.pallas_call(
        _gate_prep_kernel,
        grid=(bsz,),
        in_specs=[pl.BlockSpec((1, seq, LANES), lambda b: (b, 0, 0)),
                  pl.BlockSpec((1, LANES), lambda b: (0, 0)),
                  pl.BlockSpec((1, LANES), lambda b: (0, 0))],
        out_specs=[pl.BlockSpec((1, seq, LANES), lambda b: (b, 0, 0)),
                   pl.BlockSpec((1, nb, N_DIR * GDN_HEADS, CHUNK), lambda b: (b, 0, 0, 0))],
        out_shape=[jax.ShapeDtypeStruct((bsz, seq, LANES), F32),
                   jax.ShapeDtypeStruct((bsz, nb, N_DIR * GDN_HEADS, CHUNK), F32)],
        compiler_params=_cparams(("parallel",)),
        name="gate_prep",
    )(ab, alog, dtb)


def _conv_silu(x, w, seq):
    half = QKV_CONV // 2
    zeros = jnp.zeros((SUBLANES, x.shape[1]), F32)
    padded = jnp.concatenate([zeros, x, zeros], axis=0)
    rows = seq + 2 * SUBLANES
    acc = x * w[half:half + 1, :]
    for tap in range(QKV_CONV):
        d = tap - half
        if d == 0:
            continue
        shifted = pltpu.roll(padded, (-d) % rows, axis=0)[SUBLANES:SUBLANES + seq, :]
        acc = acc + shifted * w[tap:tap + 1, :]
    return acc * jax.nn.sigmoid(acc)


def _unit_tri_inverse(mats, ri, ci):
    eye = jnp.where(ri == ci, 1.0, 0.0)
    base = 16
    same = lambda sz: (ri >> (sz.bit_length() - 1)) == (ci >> (sz.bit_length() - 1))
    bs = [jnp.where(same(base), -a, 0.0) for a in mats]
    rs = [eye + b for b in bs]
    ps = [_mm(b, b) for b in bs]
    for _ in range(2):
        zs = [_mm(p, jnp.concatenate([p, r], axis=1)) for p, r in zip(ps, rs)]
        ps = [z[:, :CHUNK] for z in zs]
        rs = [r + z[:, CHUNK:] for r, z in zip(rs, zs)]
    ts = [r + _mm(p, r) for p, r in zip(ps, rs)]
    sz = base
    while sz < CHUNK:
        sel = jnp.logical_and(same(2 * sz), jnp.logical_not(same(sz)))
        xs = [_mm(t, jnp.where(sel, a, 0.0)) for t, a in zip(ts, mats)]
        ts = [t - _mm(x, t) for t, x in zip(ts, xs)]
        sz *= 2
    return ts


def _gdn_kernel(q_ref, k_ref, v_ref, z_ref, wq_ref, wk_ref, wv_ref, gc_ref, gt_ref, ng_ref, y_ref,
                qs, ks, vs, us, ws, qds, kdts, qks, decs, outs):
    seq = q_ref.shape[1]
    nb = seq // CHUNK
    heads = [pl.program_id(1) * GDN_HEADS_PER_STEP + hh for hh in range(GDN_HEADS_PER_STEP)]

    ri = lax.broadcasted_iota(jnp.int32, (CHUNK, CHUNK), 0)
    ci = lax.broadcasted_iota(jnp.int32, (CHUNK, CHUNK), 1)
    lane = lax.broadcasted_iota(jnp.int32, (CHUNK, LANES), 1)

    def prep(hh, i):
        chains = []
        for j in range(PREP_GROUP):
            n = i * PREP_GROUP + j
            rows = pl.ds(pl.multiple_of(n * CHUNK, CHUNK), CHUNK)
            q = qs[hh, rows, :]
            k = ks[hh, rows, :]
            v = vs[hh, rows, :]
            gc = gc_ref[0, rows, :]
            pair = []
            for d in range(N_DIR):
                col = d * GDN_HEADS + heads[hh]
                g_col = jnp.sum(jnp.where(lane == col, gc, 0.0), axis=1, keepdims=True)
                beta = jnp.sum(jnp.where(lane == N_DIR * GDN_HEADS + col, gc, 0.0), axis=1, keepdims=True)
                g_row = gt_ref[0, n, pl.ds(col, 1), :]
                incl = (ci <= ri) if d == 0 else (ci >= ri)
                decay = jnp.where(incl, jnp.exp(jnp.where(incl, g_col - g_row, 0.0)), 0.0)
                pair.append(dict(n=n, rows=rows, c=hh * N_DIR + d, d=d, q=q, k=k, v=v, g_col=g_col,
                                 beta=beta, decay=decay, k_beta=k * beta))
            prod = _mm_nt(jnp.concatenate([pair[0]["k_beta"], pair[1]["k_beta"], q], axis=0), k)
            for d, c in enumerate(pair):
                strict = (ci < ri) if d == 0 else (ci > ri)
                c["a"] = jnp.where(strict, prod[d * CHUNK:(d + 1) * CHUNK, :] * c["decay"], 0.0)
                c["qk"] = prod[N_DIR * CHUNK:, :] * c["decay"]
            chains.extend(pair)
        invs = _unit_tri_inverse([c["a"] for c in chains], ri, ci)
        for c, t in zip(chains, invs):
            ch, d, n, rows, g_col, k, q = c["c"], c["d"], c["n"], c["rows"], c["g_col"], c["k"], c["q"]
            eg = jnp.exp(g_col)
            rhs = jnp.concatenate([c["v"] * c["beta"], c["k_beta"] * eg], axis=1)
            sol = rhs + _mm(jnp.where(ri == ci, 0.0, t), rhs)
            us[ch, rows, :] = sol[:, :HEAD_DIM]
            ws[ch, rows, :] = sol[:, HEAD_DIM:].astype(BF16)
            g_last = g_col[CHUNK - 1:CHUNK, :] if d == 0 else g_col[0:1, :]
            qds[ch, rows, :] = (q * eg).astype(BF16)
            kdts[ch, n] = (k * jnp.exp(g_last - g_col)).T.astype(BF16)
            qks[ch, rows, :] = c["qk"].astype(BF16)
            decs[ch, n] = jnp.broadcast_to(jnp.exp(g_last), (SUBLANES, LANES))

    for hh in range(GDN_HEADS_PER_STEP):
        cols = slice(hh * HEAD_DIM, (hh + 1) * HEAD_DIM)
        qc = _conv_silu(q_ref[0, :, cols].astype(F32), wq_ref[:, cols], seq)
        kc = _conv_silu(k_ref[0, :, cols].astype(F32), wk_ref[:, cols], seq)
        vs[hh] = _conv_silu(v_ref[0, :, cols].astype(F32), wv_ref[:, cols], seq)
        qs[hh] = qc * lax.rsqrt(jnp.sum(qc * qc, axis=-1, keepdims=True) + NORM_EPS) * (HEAD_DIM ** -0.5)
        ks[hh] = kc * lax.rsqrt(jnp.sum(kc * kc, axis=-1, keepdims=True) + NORM_EPS)

        def prep_body(i, carry, hh=hh):
            prep(hh, i)
            return carry

        lax.fori_loop(0, nb // PREP_GROUP, prep_body, 0)

    n_chains = GDN_HEADS_PER_STEP * N_DIR

    def scan(s, states):
        chains = range(n_chains)
        ns = [s if c % N_DIR == 0 else nb - 1 - s for c in chains]
        rows = [pl.ds(pl.multiple_of(n * CHUNK, CHUNK), CHUNK) for n in ns]
        st_b = [states[c].astype(BF16) for c in chains]
        ws_st = [jnp.dot(ws[c, rows[c], :], st_b[c], preferred_element_type=F32) for c in chains]
        v_new = [(us[c, rows[c], :] - ws_st[c]).astype(BF16) for c in chains]
        upd = [jnp.dot(kdts[c, ns[c]], v_new[c], preferred_element_type=F32) for c in chains]
        new_states = tuple(states[c] * decs[c, ns[c]][0:1, :] + upd[c] for c in chains)
        for c in chains:
            outs[c, rows[c], :] = (jnp.dot(qds[c, rows[c], :], st_b[c], preferred_element_type=F32)
                                   + jnp.dot(qks[c, rows[c], :], v_new[c], preferred_element_type=F32))
        return new_states

    zero = jnp.zeros((HEAD_DIM, HEAD_DIM), F32)
    lax.fori_loop(0, nb, scan, (zero,) * n_chains)

    for hh in range(GDN_HEADS_PER_STEP):
        cols = slice(hh * HEAD_DIM, (hh + 1) * HEAD_DIM)
        o = outs[hh * N_DIR] + outs[hh * N_DIR + 1]
        zg = z_ref[0, :, cols].astype(F32)
        y = _rms(o, ng_ref[...]) * (zg * jax.nn.sigmoid(zg))
        y_ref[0, :, cols] = y.astype(y_ref.dtype)


def _gdn(qkvz, conv_w_all, layer, gc, gt, norm_g):
    bsz, seq, _ = qkvz.shape
    nb = seq // CHUNK
    hps = GDN_HEADS_PER_STEP
    width = hps * HEAD_DIM
    groups = GDN_HEADS // hps
    n_chains = hps * N_DIR
    blk = lambda part: pl.BlockSpec((1, seq, width), lambda b, g, part=part: (b, 0, part * groups + g))
    wblk = lambda part: pl.BlockSpec((None, QKV_CONV, width),
                                     lambda b, g, part=part: (layer, 0, part * groups + g))
    return pl.pallas_call(
        _gdn_kernel,
        grid=(bsz, groups),
        in_specs=[blk(0), blk(1), blk(2), blk(3), wblk(0), wblk(1), wblk(2),
                  pl.BlockSpec((1, seq, LANES), lambda b, g: (b, 0, 0)),
                  pl.BlockSpec((1, nb, N_DIR * GDN_HEADS, CHUNK), lambda b, g: (b, 0, 0, 0)),
                  pl.BlockSpec((1, HEAD_DIM), lambda b, g: (0, 0))],
        out_specs=pl.BlockSpec((1, seq, width), lambda b, g: (b, 0, g)),
        out_shape=jax.ShapeDtypeStruct((bsz, seq, GDN_WIDTH), BF16),
        scratch_shapes=[pltpu.VMEM((hps, seq, HEAD_DIM), F32),
                        pltpu.VMEM((hps, seq, HEAD_DIM), F32),
                        pltpu.VMEM((hps, seq, HEAD_DIM), F32),
                        pltpu.VMEM((n_chains, seq, HEAD_DIM), F32),
                        pltpu.VMEM((n_chains, seq, HEAD_DIM), BF16),
                        pltpu.VMEM((n_chains, seq, HEAD_DIM), BF16),
                        pltpu.VMEM((n_chains, nb, HEAD_DIM, CHUNK), BF16),
                        pltpu.VMEM((n_chains, seq, CHUNK), BF16),
                        pltpu.VMEM((n_chains, nb, SUBLANES, LANES), F32),
                        pltpu.VMEM((n_chains, seq, HEAD_DIM), F32)],
        compiler_params=_cparams(("parallel", "parallel")),
        name="gdn",
    )(qkvz, qkvz, qkvz, qkvz, conv_w_all, conv_w_all, conv_w_all, gc, gt, norm_g)


def _sgu_kernel(u_ref, v_ref, lng_ref, lnb_ref, ws_ref, bs_ref, y_ref):
    rows = u_ref.shape[1]
    u = jax.nn.gelu(u_ref[0].astype(F32), approximate=True)
    v = jax.nn.gelu(v_ref[0].astype(F32), approximate=True)
    mu = jnp.mean(v, axis=-1, keepdims=True)
    vc = v - mu
    var = jnp.mean(vc * vc, axis=-1, keepdims=True)
    vn = (vc * lax.rsqrt(var + NORM_EPS) * lng_ref[...] + lnb_ref[...]).astype(BF16)
    for blk in range(rows // SGU_BLOCK):
        r = slice(blk * SGU_BLOCK, (blk + 1) * SGU_BLOCK)
        for g in range(SGU_GROUPS):
            c = slice(g * LANES, (g + 1) * LANES)
            s = jnp.dot(ws_ref[g].astype(BF16), vn[r, c], preferred_element_type=F32) + bs_ref[g]
            y_ref[0, r, c] = (u[r, c] * s).astype(y_ref.dtype)


def _sgu(uvg, ln_g, ln_b, w_s_all, layer, b_s, ts=256):
    bsz, seq, _ = uvg.shape
    return pl.pallas_call(
        _sgu_kernel,
        grid=(bsz, seq // ts),
        in_specs=[pl.BlockSpec((1, ts, SGU_WIDTH), lambda b, i: (b, i, 0)),
                  pl.BlockSpec((1, ts, SGU_WIDTH), lambda b, i: (b, i, 1)),
                  pl.BlockSpec((1, SGU_WIDTH), lambda b, i: (0, 0)),
                  pl.BlockSpec((1, SGU_WIDTH), lambda b, i: (0, 0)),
                  pl.BlockSpec((None, SGU_GROUPS, SGU_BLOCK, SGU_BLOCK), lambda b, i: (layer, 0, 0, 0)),
                  pl.BlockSpec((SGU_GROUPS, SGU_BLOCK, LANES), lambda b, i: (0, 0, 0))],
        out_specs=pl.BlockSpec((1, ts, SGU_WIDTH), lambda b, i: (b, i, 0)),
        out_shape=jax.ShapeDtypeStruct((bsz, seq, SGU_WIDTH), BF16),
        compiler_params=_cparams(("parallel", "parallel")),
        name="sgu",
    )(uvg, uvg, ln_g, ln_b, w_s_all, b_s)


def _merge_kernel(ya_ref, yb_ref, wa_ref, wb_ref, ga_ref, gb_ref, o_ref, wa_s, wb_s):
    @pl.when(pl.program_id(1) == 0)
    def _():
        wa_s[...] = wa_ref[...].astype(BF16)
        wb_s[...] = wb_ref[...].astype(BF16)

    pa = jnp.dot(ya_ref[...], wa_s[...], preferred_element_type=F32)
    pb = jnp.dot(yb_ref[...], wb_s[...], preferred_element_type=F32)
    ga = jax.nn.sigmoid(ga_ref[...].astype(F32))
    gb = jax.nn.sigmoid(gb_ref[...].astype(F32))
    o_ref[...] = (ga * pa + gb * pb).astype(o_ref.dtype)


def _merge(ya, yb, wa_all, wb_all, layer, uvg, gate_col):
    m, ka = ya.shape
    kb = yb.shape[1]
    n = wa_all.shape[2]
    nj = n // TN
    ga0 = gate_col // TN
    return pl.pallas_call(
        _merge_kernel,
        grid=(nj, m // TM),
        in_specs=[pl.BlockSpec((TM, ka), lambda j, i: (i, 0)),
                  pl.BlockSpec((TM, kb), lambda j, i: (i, 0)),
                  pl.BlockSpec((None, ka, TN), lambda j, i: (layer, 0, j)),
                  pl.BlockSpec((None, kb, TN), lambda j, i: (layer, 0, j)),
                  pl.BlockSpec((TM, TN), lambda j, i: (i, ga0 + j)),
                  pl.BlockSpec((TM, TN), lambda j, i: (i, ga0 + nj + j))],
        out_specs=pl.BlockSpec((TM, TN), lambda j, i: (i, j)),
        out_shape=jax.ShapeDtypeStruct((m, n), BF16),
        scratch_shapes=[pltpu.VMEM((ka, TN), BF16), pltpu.VMEM((kb, TN), BF16)],
        compiler_params=_cparams(("parallel", "arbitrary")),
        name="merge",
    )(ya, yb, wa_all, wb_all, uvg, uvg)


def _proj_residual_kernel(a_ref, w_ref, x_ref, o_ref, wb_ref):
    @pl.when(pl.program_id(1) == 0)
    def _():
        wb_ref[...] = w_ref[...].astype(BF16)

    o_ref[...] = x_ref[...] + jnp.dot(a_ref[...], wb_ref[...], preferred_element_type=F32)


def _proj_residual(a, w_all, layer, x):
    m, k = a.shape
    n = w_all.shape[2]
    return pl.pallas_call(
        _proj_residual_kernel,
        grid=(n // TN, m // TM),
        in_specs=[pl.BlockSpec((TM, k), lambda j, i: (i, 0)),
                  pl.BlockSpec((None, k, TN), lambda j, i: (layer, 0, j)),
                  pl.BlockSpec((TM, TN), lambda j, i: (i, j))],
        out_specs=pl.BlockSpec((TM, TN), lambda j, i: (i, j)),
        out_shape=jax.ShapeDtypeStruct((m, n), F32),
        scratch_shapes=[pltpu.VMEM((k, TN), BF16)],
        compiler_params=_cparams(("parallel", "arbitrary")),
        name="proj_residual",
    )(a, w_all, x)


FFN_SUB = 256


def _ffn_kernel(x_ref, xp_ref, xn_ref, g_ref, wg_ref, wv_ref, cwg_ref, cwv_ref, cbg_ref, cbv_ref,
                wd_ref, o_ref, h_ref, *, tiles_per_seq):
    i = pl.program_id(0)
    f = pl.program_id(1)
    tm = x_ref.shape[0]
    tf = wg_ref.shape[1]

    @pl.when(f == 0)
    def _():
        g = g_ref[...]
        x = x_ref[...]
        h_ref[0:tm, :] = _rms(x, g).astype(BF16)
        t = i % tiles_per_seq
        hp = jnp.where(t == 0, 0.0, _rms(xp_ref[...], g))
        hn = jnp.where(t == tiles_per_seq - 1, 0.0, _rms(xn_ref[...], g))
        h_ref[tm:tm + 2 * SUBLANES, :] = jnp.concatenate([hp, hn], axis=0).astype(BF16)
        o_ref[...] = x

    h = h_ref[...]
    row = lax.broadcasted_iota(jnp.int32, (tm, FFN_SUB), 0)

    def conv(w_ref, cw_ref, cb_ref, cs):
        p = jnp.dot(h, w_ref[:, cs].astype(BF16), preferred_element_type=F32)
        main = p[0:tm, :]
        before = p[tm + SUBLANES - 1:tm + SUBLANES, :]
        after = p[tm + SUBLANES:tm + SUBLANES + 1, :]
        prev = jnp.where(row == 0, before, pltpu.roll(main, 1, axis=0))
        nxt = jnp.where(row == tm - 1, after, pltpu.roll(main, tm - 1, axis=0))
        cw = cw_ref[:, cs]
        return prev * cw[0:1, :] + main * cw[1:2, :] + nxt * cw[2:3, :] + cb_ref[:, cs]

    def up(c):
        cs = slice(c * FFN_SUB, (c + 1) * FFN_SUB)
        return conv(wg_ref, cwg_ref, cbg_ref, cs), conv(wv_ref, cwv_ref, cbv_ref, cs)

    def down(c, c_gate, c_val):
        cs = slice(c * FFN_SUB, (c + 1) * FFN_SUB)
        act = (c_gate * jax.nn.sigmoid(c_gate) * c_val).astype(BF16)
        o_ref[...] += jnp.dot(act, wd_ref[cs, :].astype(BF16), preferred_element_type=F32)

    n_sub = tf // FFN_SUB
    pending = up(0)
    for c in range(1, n_sub):
        nxt_up = up(c)
        down(c - 1, *pending)
        pending = nxt_up
    down(n_sub - 1, *pending)


def _ffn(x, g, w_up_all, conv_w_all, conv_b_all, w_down_all, layer, seq, tm=1024, tf=512):
    m, d = x.shape
    nf = D_FF // tf
    tiles_per_seq = seq // tm
    hb = tm // SUBLANES
    last = m // SUBLANES - 1
    once = pl.Buffered(1)
    return pl.pallas_call(
        functools.partial(_ffn_kernel, tiles_per_seq=tiles_per_seq),
        grid=(m // tm, nf),
        in_specs=[pl.BlockSpec((tm, d), lambda i, f: (i, 0), pipeline_mode=once),
                  pl.BlockSpec((SUBLANES, d), lambda i, f: (jnp.maximum(i * hb - 1, 0), 0)),
                  pl.BlockSpec((SUBLANES, d), lambda i, f: (jnp.minimum((i + 1) * hb, last), 0)),
                  pl.BlockSpec((1, d), lambda i, f: (0, 0)),
                  pl.BlockSpec((None, d, tf), lambda i, f: (layer, 0, f)),
                  pl.BlockSpec((None, d, tf), lambda i, f, nf=nf: (layer, 0, nf + f)),
                  pl.BlockSpec((None, 3, tf), lambda i, f: (layer, 0, f)),
                  pl.BlockSpec((None, 3, tf), lambda i, f, nf=nf: (layer, 0, nf + f)),
                  pl.BlockSpec((None, 1, tf), lambda i, f: (layer, 0, f)),
                  pl.BlockSpec((None, 1, tf), lambda i, f, nf=nf: (layer, 0, nf + f)),
                  pl.BlockSpec((None, tf, d), lambda i, f: (layer, f, 0))],
        out_specs=pl.BlockSpec((tm, d), lambda i, f: (i, 0), pipeline_mode=once),
        out_shape=jax.ShapeDtypeStruct((m, d), F32),
        scratch_shapes=[pltpu.VMEM((tm + 2 * SUBLANES, d), BF16)],
        compiler_params=_cparams(("parallel", "arbitrary")),
        name="ffn",
    )(x, x, x, g, w_up_all, w_up_all, conv_w_all, conv_w_all, conv_b_all, conv_b_all, w_down_all)


def _pad_lanes(v):
    v = v.reshape(1, -1).astype(F32)
    return jnp.pad(v, ((0, 0), (0, LANES - v.shape[1])))


def kernel(x, norm_mix_g, w_in, qkv_conv_w, a_log, dt_bias, gdn_norm_g, w_branch_a, sgu_ln_g, sgu_ln_b,
           sgu_w, sgu_b, w_branch_b, w_out, norm_ffn_g, w_up, ffn_conv_w, ffn_conv_b, w_down, final_norm_g):
    bsz, seq, d = x.shape
    depth = w_in.shape[0]
    m = bsz * seq
    n_uvg = w_in.shape[2] - COL_UV
    w_ab = jnp.pad(w_in[:, :, COL_AB:COL_UV], ((0, 0), (0, 0), (0, LANES - N_AB))).astype(BF16)
    conv_b = ffn_conv_b.reshape(depth, 1, 2 * D_FF)
    xf = x.reshape(m, d)
    for l in range(depth):
        h = _rmsnorm(xf, norm_mix_g[l].reshape(1, d), BF16)
        qkvz = _proj(h, w_in, l, COL_AB, BF16)
        uvg = _proj_shifted(h, w_in, l, COL_UV, n_uvg, BF16)
        ab = _small_proj(h, w_ab[l])

        gc, gt = _gate_prep(ab.reshape(bsz, seq, LANES), _pad_lanes(a_log[l]), _pad_lanes(dt_bias[l]))
        y_a = _gdn(qkvz.reshape(bsz, seq, COL_AB), qkv_conv_w, l, gc, gt,
                   gdn_norm_g[l].reshape(1, HEAD_DIM))
        b_s = jnp.broadcast_to(sgu_b[l][:, :, None], (SGU_GROUPS, SGU_BLOCK, LANES))
        y_b = _sgu(uvg.reshape(bsz, seq, n_uvg), sgu_ln_g[l].reshape(1, SGU_WIDTH),
                   sgu_ln_b[l].reshape(1, SGU_WIDTH), sgu_w, l, b_s)

        merged = _merge(y_a.reshape(m, GDN_WIDTH), y_b.reshape(m, SGU_WIDTH),
                        w_branch_a, w_branch_b, l, uvg, 2 * SGU_WIDTH)
        xf = _proj_residual(merged, w_out, l, xf)

        xf = _ffn(xf, norm_ffn_g[l].reshape(1, d), w_up, ffn_conv_w, conv_b, w_down, l, seq)
    out = _rmsnorm(xf, final_norm_g.reshape(1, d), F32)
    return out.reshape(bsz, seq, d)
```

```python
import functools

import jax
import jax.numpy as jnp
from jax import lax
from jax.experimental import pallas as pl
from jax.experimental.pallas import tpu as pltpu

F32 = jnp.float32
BF16 = jnp.bfloat16

D_MODEL = 2048
GDN_HEADS = 8
HEAD_DIM = 128
GDN_WIDTH = GDN_HEADS * HEAD_DIM
QKV_CONV = 5
N_DIR = 2
SGU_GROUPS = 8
SGU_WIDTH = 1024
SGU_BLOCK = 128
D_FF = 5632
NORM_EPS = 1e-6
N_AB = 2 * N_DIR * GDN_HEADS
COL_AB = 4 * GDN_WIDTH
COL_UV = COL_AB + N_AB

LANES = 128
SUBLANES = 8
GDN_HEADS_PER_STEP = 2
PREP_GROUP = 8
CHUNK = 128
VMEM_LIMIT = 56 * 1024 * 1024
TM = 1024
TN = 1024


def _cparams(sem):
    return pltpu.CompilerParams(dimension_semantics=sem, vmem_limit_bytes=VMEM_LIMIT)


def _mm(a, b):
    return jnp.dot(a.astype(BF16), b.astype(BF16), preferred_element_type=F32)


def _mm_nt(a, b):
    return lax.dot_general(a.astype(BF16), b.astype(BF16), (((1,), (1,)), ((), ())),
                           preferred_element_type=F32)


def _rms(x, g):
    return x * lax.rsqrt(jnp.mean(x * x, axis=-1, keepdims=True) + NORM_EPS) * g


def _rmsnorm_kernel(x_ref, g_ref, o_ref):
    o_ref[...] = _rms(x_ref[...], g_ref[...]).astype(o_ref.dtype)


def _rmsnorm(x, g, out_dtype, tm=512):
    m, d = x.shape
    return pl.pallas_call(
        _rmsnorm_kernel,
        grid=(m // tm,),
        in_specs=[pl.BlockSpec((tm, d), lambda i: (i, 0)),
                  pl.BlockSpec((1, d), lambda i: (0, 0))],
        out_specs=pl.BlockSpec((tm, d), lambda i: (i, 0)),
        out_shape=jax.ShapeDtypeStruct((m, d), out_dtype),
        compiler_params=_cparams(("parallel",)),
        name="rmsnorm",
    )(x, g)


def _proj_nt_kernel(a_ref, w_ref, o_ref, wb_ref):
    @pl.when(pl.program_id(1) == 0)
    def _():
        wb_ref[...] = w_ref[0].astype(BF16)

    o_ref[...] = _mm_nt(a_ref[...], wb_ref[...]).astype(o_ref.dtype)


def _proj_nt(a, wt_all, layer, row0, n_rows, out_dtype):
    m, k = a.shape
    assert row0 % SUBLANES == 0 and n_rows % TN == 0
    return pl.pallas_call(
        _proj_nt_kernel,
        grid=(n_rows // TN, m // TM),
        in_specs=[pl.BlockSpec((TM, k), lambda j, i: (i, 0)),
                  pl.BlockSpec((pl.Element(1), pl.Element(TN), pl.Element(k)),
                               lambda j, i: (layer, pl.multiple_of(row0 + j * TN, SUBLANES), 0))],
        out_specs=pl.BlockSpec((TM, TN), lambda j, i: (i, j)),
        out_shape=jax.ShapeDtypeStruct((m, n_rows), out_dtype),
        scratch_shapes=[pltpu.VMEM((TN, k), BF16)],
        compiler_params=_cparams(("parallel", "arbitrary")),
        name="proj_nt",
    )(a, wt_all)


def _small_proj_kernel(a_ref, w_ref, o_ref):
    o_ref[...] = _mm_nt(a_ref[...], w_ref[...])


def _small_proj(a, wt_all, layer, row0):
    m, k = a.shape
    assert row0 % LANES == 0
    return pl.pallas_call(
        _small_proj_kernel,
        grid=(m // TM,),
        in_specs=[pl.BlockSpec((TM, k), lambda i: (i, 0)),
                  pl.BlockSpec((None, LANES, k), lambda i: (layer, row0 // LANES, 0))],
        out_specs=pl.BlockSpec((TM, LANES), lambda i: (i, 0)),
        out_shape=jax.ShapeDtypeStruct((m, LANES), F32),
        compiler_params=_cparams(("parallel",)),
        name="small_proj",
    )(a, wt_all)


def _split3(x):
    hi = x.astype(BF16)
    r1 = x - hi.astype(F32)
    mid = r1.astype(BF16)
    lo = (r1 - mid.astype(F32)).astype(BF16)
    return hi, mid, lo


def _gate_prep_kernel(ab_ref, alog_ref, dtb_ref, gc_ref, gt_ref):
    seq = ab_ref.shape[1]
    lane = lax.broadcasted_iota(jnp.int32, (CHUNK, LANES), 1)
    ri = lax.broadcasted_iota(jnp.int32, (CHUNK, CHUNK), 0)
    ci = lax.broadcasted_iota(jnp.int32, (CHUNK, CHUNK), 1)
    lower = jnp.where(ci <= ri, 1.0, 0.0).astype(BF16)
    upper = jnp.where(ci >= ri, 1.0, 0.0).astype(BF16)
    neg_a = -jnp.exp(alog_ref[...])
    dtb = dtb_ref[...]
    for n in range(seq // CHUNK):
        ab = ab_ref[0, n * CHUNK:(n + 1) * CHUNK, :]
        z = ab + dtb
        softplus = jnp.maximum(z, 0.0) + jnp.log1p(jnp.exp(-jnp.abs(z)))
        g = neg_a * softplus
        beta = jax.nn.sigmoid(ab)
        pre = jnp.zeros((CHUNK, LANES), F32)
        suf = jnp.zeros((CHUNK, LANES), F32)
        for part in _split3(g):
            pre = pre + jnp.dot(lower, part, preferred_element_type=F32)
            suf = suf + jnp.dot(upper, part, preferred_element_type=F32)
        gsum = jnp.where(lane < GDN_HEADS, pre, suf)
        gc_ref[0, n * CHUNK:(n + 1) * CHUNK, :] = jnp.where(lane < N_DIR * GDN_HEADS, gsum, beta)
        gt_ref[0, n] = gsum.T[0:N_DIR * GDN_HEADS, :]


def _gate_prep(ab, alog, dtb):
    bsz, seq, _ = ab.shape
    nb = seq // CHUNK
    return pl.pallas_call(
        _gate_prep_kernel,
        grid=(bsz,),
        in_specs=[pl.BlockSpec((1, seq, LANES), lambda b: (b, 0, 0)),
                  pl.BlockSpec((1, LANES), lambda b: (0, 0)),
                  pl.BlockSpec((1, LANES), lambda b: (0, 0))],
        out_specs=[pl.BlockSpec((1, seq, LANES), lambda b: (b, 0, 0)),
                   pl.BlockSpec((1, nb, N_DIR * GDN_HEADS, CHUNK), lambda b: (b, 0, 0, 0))],
        out_shape=[jax.ShapeDtypeStruct((bsz, seq, LANES), F32),
                   jax.ShapeDtypeStruct((bsz, nb, N_DIR * GDN_HEADS, CHUNK), F32)],
        compiler_params=_cparams(("parallel",)),
        name="gate_prep",
    )(ab, alog, dtb)


CONV_PAD = 16


def _conv_silu_rows(x_ref, cols, w, r0, r1, seq):
    lo, hi = max(r0 - CONV_PAD, 0), min(r1 + CONV_PAD, seq)
    parts = [x_ref[0, lo:hi, cols].astype(F32)]
    if lo > r0 - CONV_PAD:
        parts.insert(0, jnp.zeros((CONV_PAD, parts[0].shape[1]), F32))
    if hi < r1 + CONV_PAD:
        parts.append(jnp.zeros((CONV_PAD, parts[0].shape[1]), F32))
    window = jnp.concatenate(parts, axis=0) if len(parts) > 1 else parts[0]
    rows = r1 - r0 + 2 * CONV_PAD
    half = QKV_CONV // 2
    acc = None
    for tap in range(QKV_CONV):
        d = tap - half
        shifted = window if d == 0 else pltpu.roll(window, (-d) % rows, axis=0)
        term = shifted[CONV_PAD:CONV_PAD + r1 - r0, :] * w[tap:tap + 1, :]
        acc = term if acc is None else acc + term
    return acc * jax.nn.sigmoid(acc)


def _unit_tri_inverse(mats, ri, ci):
    eye = jnp.where(ri == ci, 1.0, 0.0)
    base = 16
    same = lambda sz: (ri >> (sz.bit_length() - 1)) == (ci >> (sz.bit_length() - 1))
    bs = [jnp.where(same(base), -a, 0.0) for a in mats]
    rs = [eye + b for b in bs]
    ps = [_mm(b, b) for b in bs]
    for _ in range(2):
        zs = [_mm(p, jnp.concatenate([p, r], axis=1)) for p, r in zip(ps, rs)]
        ps = [z[:, :CHUNK] for z in zs]
        rs = [r + z[:, CHUNK:] for r, z in zip(rs, zs)]
    ts = [r + _mm(p, r) for p, r in zip(ps, rs)]
    sz = base
    while sz < CHUNK:
        sel = jnp.logical_and(same(2 * sz), jnp.logical_not(same(sz)))
        xs = [_mm(t, jnp.where(sel, a, 0.0)) for t, a in zip(ts, mats)]
        ts = [t - _mm(x, t) for t, x in zip(ts, xs)]
        sz *= 2
    return ts


def _gdn_kernel(q_ref, k_ref, v_ref, z_ref, wq_ref, wk_ref, wv_ref, gc_ref, gt_ref, ng_ref, y_ref,
                qs, ks, vs, us, ws, qds, kdts, qks, decs, outs):
    seq = q_ref.shape[1]
    nb = seq // CHUNK
    heads = [pl.program_id(1) * GDN_HEADS_PER_STEP + hh for hh in range(GDN_HEADS_PER_STEP)]

    ri = lax.broadcasted_iota(jnp.int32, (CHUNK, CHUNK), 0)
    ci = lax.broadcasted_iota(jnp.int32, (CHUNK, CHUNK), 1)
    lane = lax.broadcasted_iota(jnp.int32, (CHUNK, LANES), 1)

    def conv_group(hh, i):
        r0, r1 = i * PREP_GROUP * CHUNK, (i + 1) * PREP_GROUP * CHUNK
        cols = slice(hh * HEAD_DIM, (hh + 1) * HEAD_DIM)
        qc = _conv_silu_rows(q_ref, cols, wq_ref[:, cols], r0, r1, seq)
        kc = _conv_silu_rows(k_ref, cols, wk_ref[:, cols], r0, r1, seq)
        vs[hh, r0:r1, :] = _conv_silu_rows(v_ref, cols, wv_ref[:, cols], r0, r1, seq)
        qs[hh, r0:r1, :] = (qc * lax.rsqrt(jnp.sum(qc * qc, axis=-1, keepdims=True) + NORM_EPS)
                            * (HEAD_DIM ** -0.5))
        ks[hh, r0:r1, :] = kc * lax.rsqrt(jnp.sum(kc * kc, axis=-1, keepdims=True) + NORM_EPS)

    def prep(hh, i, after_first_stage):
        chains = []
        for j in range(PREP_GROUP):
            n = i * PREP_GROUP + j
            rows = slice(n * CHUNK, (n + 1) * CHUNK)
            q = qs[hh, rows, :]
            k = ks[hh, rows, :]
            v = vs[hh, rows, :]
            gc = gc_ref[0, rows, :]
            pair = []
            for d in range(N_DIR):
                col = d * GDN_HEADS + heads[hh]
                g_col = jnp.sum(jnp.where(lane == col, gc, 0.0), axis=1, keepdims=True)
                beta = jnp.sum(jnp.where(lane == N_DIR * GDN_HEADS + col, gc, 0.0), axis=1, keepdims=True)
                g_row = gt_ref[0, n, pl.ds(col, 1), :]
                incl = (ci <= ri) if d == 0 else (ci >= ri)
                decay = jnp.where(incl, jnp.exp(jnp.where(incl, g_col - g_row, 0.0)), 0.0)
                pair.append(dict(n=n, rows=rows, c=hh * N_DIR + d, d=d, q=q, k=k, v=v, g_col=g_col,
                                 beta=beta, decay=decay, k_beta=k * beta))
            prod = _mm_nt(jnp.concatenate([pair[0]["k_beta"], pair[1]["k_beta"], q], axis=0), k)
            for d, c in enumerate(pair):
                strict = (ci < ri) if d == 0 else (ci > ri)
                c["a"] = jnp.where(strict, prod[d * CHUNK:(d + 1) * CHUNK, :] * c["decay"], 0.0)
                c["qk"] = prod[N_DIR * CHUNK:, :] * c["decay"]
            chains.extend(pair)
        after_first_stage()
        invs = _unit_tri_inverse([c["a"] for c in chains], ri, ci)
        for c, t in zip(chains, invs):
            ch, d, n, rows, g_col, k, q = c["c"], c["d"], c["n"], c["rows"], c["g_col"], c["k"], c["q"]
            eg = jnp.exp(g_col)
            rhs = jnp.concatenate([c["v"] * c["beta"], c["k_beta"] * eg], axis=1)
            sol = rhs + _mm(jnp.where(ri == ci, 0.0, t), rhs)
            us[ch, rows, :] = sol[:, :HEAD_DIM]
            ws[ch, rows, :] = sol[:, HEAD_DIM:].astype(BF16)
            g_last = g_col[CHUNK - 1:CHUNK, :] if d == 0 else g_col[0:1, :]
            qds[ch, rows, :] = (q * eg).astype(BF16)
            kdts[ch, n] = (k * jnp.exp(g_last - g_col)).T.astype(BF16)
            qks[ch, rows, :] = c["qk"].astype(BF16)
            decs[ch, n] = jnp.broadcast_to(jnp.exp(g_last), (SUBLANES, LANES))

    items = [(hh, i) for hh in range(GDN_HEADS_PER_STEP) for i in range(nb // PREP_GROUP)]
    conv_group(*items[0])
    for idx, (hh, i) in enumerate(items):
        nxt = items[idx + 1] if idx + 1 < len(items) else None
        prep(hh, i, (lambda nxt=nxt: conv_group(*nxt)) if nxt else (lambda: None))

    n_chains = GDN_HEADS_PER_STEP * N_DIR

    def scan(s, states):
        chains = range(n_chains)
        ns = [s if c % N_DIR == 0 else nb - 1 - s for c in chains]
        rows = [pl.ds(pl.multiple_of(n * CHUNK, CHUNK), CHUNK) for n in ns]
        st_b = [states[c].astype(BF16) for c in chains]
        ws_st = [jnp.dot(ws[c, rows[c], :], st_b[c], preferred_element_type=F32) for c in chains]
        v_new = [(us[c, rows[c], :] - ws_st[c]).astype(BF16) for c in chains]
        upd = [jnp.dot(kdts[c, ns[c]], v_new[c], preferred_element_type=F32) for c in chains]
        new_states = tuple(states[c] * decs[c, ns[c]][0:1, :] + upd[c] for c in chains)
        for c in chains:
            outs[c, rows[c], :] = (jnp.dot(qds[c, rows[c], :], st_b[c], preferred_element_type=F32)
                                   + jnp.dot(qks[c, rows[c], :], v_new[c], preferred_element_type=F32))
        return new_states

    zero = jnp.zeros((HEAD_DIM, HEAD_DIM), F32)
    lax.fori_loop(0, nb, scan, (zero,) * n_chains)

    for hh in range(GDN_HEADS_PER_STEP):
        cols = slice(hh * HEAD_DIM, (hh + 1) * HEAD_DIM)
        o = outs[hh * N_DIR] + outs[hh * N_DIR + 1]
        zg = z_ref[0, :, cols].astype(F32)
        y = _rms(o, ng_ref[...]) * (zg * jax.nn.sigmoid(zg))
        y_ref[0, :, cols] = y.astype(y_ref.dtype)


def _gdn(qkvz, conv_w_all, layer, gc, gt, norm_g):
    bsz, seq, _ = qkvz.shape
    nb = seq // CHUNK
    hps = GDN_HEADS_PER_STEP
    width = hps * HEAD_DIM
    groups = GDN_HEADS // hps
    n_chains = hps * N_DIR
    blk = lambda part: pl.BlockSpec((1, seq, width), lambda b, g, part=part: (b, 0, part * groups + g))
    wblk = lambda part: pl.BlockSpec((None, QKV_CONV, width),
                                     lambda b, g, part=part: (layer, 0, part * groups + g))
    return pl.pallas_call(
        _gdn_kernel,
        grid=(bsz, groups),
        in_specs=[blk(0), blk(1), blk(2), blk(3), wblk(0), wblk(1), wblk(2),
                  pl.BlockSpec((1, seq, LANES), lambda b, g: (b, 0, 0)),
                  pl.BlockSpec((1, nb, N_DIR * GDN_HEADS, CHUNK), lambda b, g: (b, 0, 0, 0)),
                  pl.BlockSpec((1, HEAD_DIM), lambda b, g: (0, 0))],
        out_specs=pl.BlockSpec((1, seq, width), lambda b, g: (b, 0, g)),
        out_shape=jax.ShapeDtypeStruct((bsz, seq, GDN_WIDTH), BF16),
        scratch_shapes=[pltpu.VMEM((hps, seq, HEAD_DIM), F32),
                        pltpu.VMEM((hps, seq, HEAD_DIM), F32),
                        pltpu.VMEM((hps, seq, HEAD_DIM), F32),
                        pltpu.VMEM((n_chains, seq, HEAD_DIM), F32),
                        pltpu.VMEM((n_chains, seq, HEAD_DIM), BF16),
                        pltpu.VMEM((n_chains, seq, HEAD_DIM), BF16),
                        pltpu.VMEM((n_chains, nb, HEAD_DIM, CHUNK), BF16),
                        pltpu.VMEM((n_chains, seq, CHUNK), BF16),
                        pltpu.VMEM((n_chains, nb, SUBLANES, LANES), F32),
                        pltpu.VMEM((n_chains, seq, HEAD_DIM), F32)],
        compiler_params=_cparams(("parallel", "parallel")),
        name="gdn",
    )(qkvz, qkvz, qkvz, qkvz, conv_w_all, conv_w_all, conv_w_all, gc, gt, norm_g)


def _sgu_kernel(u_ref, v_ref, lng_ref, lnb_ref, ws_ref, bs_ref, y_ref):
    rows = u_ref.shape[1]
    u = jax.nn.gelu(u_ref[0].astype(F32), approximate=True)
    v = jax.nn.gelu(v_ref[0].astype(F32), approximate=True)
    mu = jnp.mean(v, axis=-1, keepdims=True)
    vc = v - mu
    var = jnp.mean(vc * vc, axis=-1, keepdims=True)
    vn = (vc * lax.rsqrt(var + NORM_EPS) * lng_ref[...] + lnb_ref[...]).astype(BF16)
    for blk in range(rows // SGU_BLOCK):
        r = slice(blk * SGU_BLOCK, (blk + 1) * SGU_BLOCK)
        for g in range(SGU_GROUPS):
            c = slice(g * LANES, (g + 1) * LANES)
            s = jnp.dot(ws_ref[g].astype(BF16), vn[r, c], preferred_element_type=F32) + bs_ref[g]
            y_ref[0, r, c] = (u[r, c] * s).astype(y_ref.dtype)


def _sgu(uvg, ln_g, ln_b, w_s_all, layer, b_s, ts=256):
    bsz, seq, _ = uvg.shape
    return pl.pallas_call(
        _sgu_kernel,
        grid=(bsz, seq // ts),
        in_specs=[pl.BlockSpec((1, ts, SGU_WIDTH), lambda b, i: (b, i, 0)),
                  pl.BlockSpec((1, ts, SGU_WIDTH), lambda b, i: (b, i, 1)),
                  pl.BlockSpec((1, SGU_WIDTH), lambda b, i: (0, 0)),
                  pl.BlockSpec((1, SGU_WIDTH), lambda b, i: (0, 0)),
                  pl.BlockSpec((None, SGU_GROUPS, SGU_BLOCK, SGU_BLOCK), lambda b, i: (layer, 0, 0, 0)),
                  pl.BlockSpec((SGU_GROUPS, SGU_BLOCK, LANES), lambda b, i: (0, 0, 0))],
        out_specs=pl.BlockSpec((1, ts, SGU_WIDTH), lambda b, i: (b, i, 0)),
        out_shape=jax.ShapeDtypeStruct((bsz, seq, SGU_WIDTH), BF16),
        compiler_params=_cparams(("parallel", "parallel")),
        name="sgu",
    )(uvg, uvg, ln_g, ln_b, w_s_all, b_s)


def _merge_kernel(ya_ref, yb_ref, wa_ref, wb_ref, ga_ref, gb_ref, o_ref, wa_s, wb_s):
    @pl.when(pl.program_id(1) == 0)
    def _():
        wa_s[...] = wa_ref[...].astype(BF16)
        wb_s[...] = wb_ref[...].astype(BF16)

    pa = jnp.dot(ya_ref[...], wa_s[...], preferred_element_type=F32)
    pb = jnp.dot(yb_ref[...], wb_s[...], preferred_element_type=F32)
    ga = jax.nn.sigmoid(ga_ref[...].astype(F32))
    gb = jax.nn.sigmoid(gb_ref[...].astype(F32))
    o_ref[...] = (ga * pa + gb * pb).astype(o_ref.dtype)


def _merge(ya, yb, wa_all, wb_all, layer, uvg, gate_col):
    m, ka = ya.shape
    kb = yb.shape[1]
    n = wa_all.shape[2]
    nj = n // TN
    ga0 = gate_col // TN
    return pl.pallas_call(
        _merge_kernel,
        grid=(nj, m // TM),
        in_specs=[pl.BlockSpec((TM, ka), lambda j, i: (i, 0)),
                  pl.BlockSpec((TM, kb), lambda j, i: (i, 0)),
                  pl.BlockSpec((None, ka, TN), lambda j, i: (layer, 0, j)),
                  pl.BlockSpec((None, kb, TN), lambda j, i: (layer, 0, j)),
                  pl.BlockSpec((TM, TN), lambda j, i: (i, ga0 + j)),
                  pl.BlockSpec((TM, TN), lambda j, i: (i, ga0 + nj + j))],
        out_specs=pl.BlockSpec((TM, TN), lambda j, i: (i, j)),
        out_shape=jax.ShapeDtypeStruct((m, n), BF16),
        scratch_shapes=[pltpu.VMEM((ka, TN), BF16), pltpu.VMEM((kb, TN), BF16)],
        compiler_params=_cparams(("parallel", "arbitrary")),
        name="merge",
    )(ya, yb, wa_all, wb_all, uvg, uvg)


def _proj_residual_kernel(a_ref, w_ref, x_ref, o_ref, wb_ref):
    @pl.when(pl.program_id(1) == 0)
    def _():
        wb_ref[...] = w_ref[...].astype(BF16)

    o_ref[...] = x_ref[...] + jnp.dot(a_ref[...], wb_ref[...], preferred_element_type=F32)


def _proj_residual(a, w_all, layer, x):
    m, k = a.shape
    n = w_all.shape[2]
    return pl.pallas_call(
        _proj_residual_kernel,
        grid=(n // TN, m // TM),
        in_specs=[pl.BlockSpec((TM, k), lambda j, i: (i, 0)),
                  pl.BlockSpec((None, k, TN), lambda j, i: (layer, 0, j)),
                  pl.BlockSpec((TM, TN), lambda j, i: (i, j))],
        out_specs=pl.BlockSpec((TM, TN), lambda j, i: (i, j)),
        out_shape=jax.ShapeDtypeStruct((m, n), F32),
        scratch_shapes=[pltpu.VMEM((k, TN), BF16)],
        compiler_params=_cparams(("parallel", "arbitrary")),
        name="proj_residual",
    )(a, w_all, x)


FFN_SUB = 256


def _ffn_kernel(x_ref, xp_ref, xn_ref, g_ref, wg_ref, wv_ref, cwg_ref, cwv_ref, cbg_ref, cbv_ref,
                wd_ref, o_ref, h_ref, *, tiles_per_seq):
    i = pl.program_id(0)
    f = pl.program_id(1)
    tm = x_ref.shape[0]
    tf = wg_ref.shape[1]

    @pl.when(f == 0)
    def _():
        g = g_ref[...]
        x = x_ref[...]
        h_ref[0:tm, :] = _rms(x, g).astype(BF16)
        t = i % tiles_per_seq
        hp = jnp.where(t == 0, 0.0, _rms(xp_ref[...], g))
        hn = jnp.where(t == tiles_per_seq - 1, 0.0, _rms(xn_ref[...], g))
        h_ref[tm:tm + 2 * SUBLANES, :] = jnp.concatenate([hp, hn], axis=0).astype(BF16)
        o_ref[...] = x

    h = h_ref[...]
    row = lax.broadcasted_iota(jnp.int32, (tm, FFN_SUB), 0)

    def conv(w_ref, cw_ref, cb_ref, cs):
        p = jnp.dot(h, w_ref[:, cs].astype(BF16), preferred_element_type=F32)
        main = p[0:tm, :]
        before = p[tm + SUBLANES - 1:tm + SUBLANES, :]
        after = p[tm + SUBLANES:tm + SUBLANES + 1, :]
        prev = jnp.where(row == 0, before, pltpu.roll(main, 1, axis=0))
        nxt = jnp.where(row == tm - 1, after, pltpu.roll(main, tm - 1, axis=0))
        cw = cw_ref[:, cs]
        return prev * cw[0:1, :] + main * cw[1:2, :] + nxt * cw[2:3, :] + cb_ref[:, cs]

    def up(c):
        cs = slice(c * FFN_SUB, (c + 1) * FFN_SUB)
        return conv(wg_ref, cwg_ref, cbg_ref, cs), conv(wv_ref, cwv_ref, cbv_ref, cs)

    def down(c, c_gate, c_val):
        cs = slice(c * FFN_SUB, (c + 1) * FFN_SUB)
        act = (c_gate * jax.nn.sigmoid(c_gate) * c_val).astype(BF16)
        o_ref[...] += jnp.dot(act, wd_ref[cs, :].astype(BF16), preferred_element_type=F32)

    n_sub = tf // FFN_SUB
    pending = up(0)
    for c in range(1, n_sub):
        nxt_up = up(c)
        down(c - 1, *pending)
        pending = nxt_up
    down(n_sub - 1, *pending)


def _ffn(x, g, w_up_all, conv_w_all, conv_b_all, w_down_all, layer, seq, tm=1024, tf=512):
    m, d = x.shape
    nf = D_FF // tf
    tiles_per_seq = seq // tm
    hb = tm // SUBLANES
    last = m // SUBLANES - 1
    once = pl.Buffered(1)
    return pl.pallas_call(
        functools.partial(_ffn_kernel, tiles_per_seq=tiles_per_seq),
        grid=(m // tm, nf),
        in_specs=[pl.BlockSpec((tm, d), lambda i, f: (i, 0), pipeline_mode=once),
                  pl.BlockSpec((SUBLANES, d), lambda i, f: (jnp.maximum(i * hb - 1, 0), 0)),
                  pl.BlockSpec((SUBLANES, d), lambda i, f: (jnp.minimum((i + 1) * hb, last), 0)),
                  pl.BlockSpec((1, d), lambda i, f: (0, 0)),
                  pl.BlockSpec((None, d, tf), lambda i, f: (layer, 0, f)),
                  pl.BlockSpec((None, d, tf), lambda i, f, nf=nf: (layer, 0, nf + f)),
                  pl.BlockSpec((None, 3, tf), lambda i, f: (layer, 0, f)),
                  pl.BlockSpec((None, 3, tf), lambda i, f, nf=nf: (layer, 0, nf + f)),
                  pl.BlockSpec((None, 1, tf), lambda i, f: (layer, 0, f)),
                  pl.BlockSpec((None, 1, tf), lambda i, f, nf=nf: (layer, 0, nf + f)),
                  pl.BlockSpec((None, tf, d), lambda i, f: (layer, f, 0))],
        out_specs=pl.BlockSpec((tm, d), lambda i, f: (i, 0), pipeline_mode=once),
        out_shape=jax.ShapeDtypeStruct((m, d), F32),
        scratch_shapes=[pltpu.VMEM((tm + 2 * SUBLANES, d), BF16)],
        compiler_params=_cparams(("parallel", "arbitrary")),
        name="ffn",
    )(x, x, x, g, w_up_all, w_up_all, conv_w_all, conv_w_all, conv_b_all, conv_b_all, w_down_all)


def _pad_lanes(v):
    v = v.reshape(1, -1).astype(F32)
    return jnp.pad(v, ((0, 0), (0, LANES - v.shape[1])))


def kernel(x, norm_mix_g, w_in, qkv_conv_w, a_log, dt_bias, gdn_norm_g, w_branch_a, sgu_ln_g, sgu_ln_b,
           sgu_w, sgu_b, w_branch_b, w_out, norm_ffn_g, w_up, ffn_conv_w, ffn_conv_b, w_down, final_norm_g):
    bsz, seq, d = x.shape
    depth = w_in.shape[0]
    m = bsz * seq
    n_uvg = w_in.shape[2] - COL_UV
    conv_b = ffn_conv_b.reshape(depth, 1, 2 * D_FF)
    w_in_t = jnp.swapaxes(w_in, 1, 2)
    xf = x.reshape(m, d)
    for l in range(depth):
        h = _rmsnorm(xf, norm_mix_g[l].reshape(1, d), BF16)
        qkvz = _proj_nt(h, w_in_t, l, 0, COL_AB, BF16)
        uvg = _proj_nt(h, w_in_t, l, COL_UV, n_uvg, BF16)
        ab = _small_proj(h, w_in_t, l, COL_AB)

        gc, gt = _gate_prep(ab.reshape(bsz, seq, LANES), _pad_lanes(a_log[l]), _pad_lanes(dt_bias[l]))
        y_a = _gdn(qkvz.reshape(bsz, seq, COL_AB), qkv_conv_w, l, gc, gt,
                   gdn_norm_g[l].reshape(1, HEAD_DIM))
        b_s = jnp.broadcast_to(sgu_b[l][:, :, None], (SGU_GROUPS, SGU_BLOCK, LANES))
        y_b = _sgu(uvg.reshape(bsz, seq, n_uvg), sgu_ln_g[l].reshape(1, SGU_WIDTH),
                   sgu_ln_b[l].reshape(1, SGU_WIDTH), sgu_w, l, b_s)

        merged = _merge(y_a.reshape(m, GDN_WIDTH), y_b.reshape(m, SGU_WIDTH),
                        w_branch_a, w_branch_b, l, uvg, 2 * SGU_WIDTH)
        xf = _proj_residual(merged, w_out, l, xf)

        xf = _ffn(xf, norm_ffn_g[l].reshape(1, d), w_up, ffn_conv_w, conv_b, w_down, l, seq)
    out = _rmsnorm(xf, final_norm_g.reshape(1, d), F32)
    return out.reshape(bsz, seq, d)
```

```python
import functools

import jax
import jax.numpy as jnp
from jax import lax
from jax.experimental import pallas as pl
from jax.experimental.pallas import tpu as pltpu

F32 = jnp.float32
BF16 = jnp.bfloat16

D_MODEL = 2048
GDN_HEADS = 8
HEAD_DIM = 128
GDN_WIDTH = GDN_HEADS * HEAD_DIM
QKV_CONV = 5
N_DIR = 2
SGU_GROUPS = 8
SGU_WIDTH = 1024
SGU_BLOCK = 128
D_FF = 5632
NORM_EPS = 1e-6
N_AB = 2 * N_DIR * GDN_HEADS
COL_AB = 4 * GDN_WIDTH
COL_UV = COL_AB + N_AB

LANES = 128
SUBLANES = 8
GDN_HEADS_PER_STEP = 2
PREP_GROUP = 8
CHUNK = 128
VMEM_LIMIT = 56 * 1024 * 1024
TM = 1024
TN = 1024


def _cparams(sem):
    return pltpu.CompilerParams(dimension_semantics=sem, vmem_limit_bytes=VMEM_LIMIT)


def _mm(a, b):
    return jnp.dot(a.astype(BF16), b.astype(BF16), preferred_element_type=F32)


def _mm_nt(a, b):
    return lax.dot_general(a.astype(BF16), b.astype(BF16), (((1,), (1,)), ((), ())),
                           preferred_element_type=F32)


def _rms(x, g):
    return x * lax.rsqrt(jnp.mean(x * x, axis=-1, keepdims=True) + NORM_EPS) * g


def _rmsnorm_kernel(x_ref, g_ref, o_ref):
    o_ref[...] = _rms(x_ref[...], g_ref[...]).astype(o_ref.dtype)


def _rmsnorm(x, g, out_dtype, tm=512):
    m, d = x.shape
    return pl.pallas_call(
        _rmsnorm_kernel,
        grid=(m // tm,),
        in_specs=[pl.BlockSpec((tm, d), lambda i: (i, 0)),
                  pl.BlockSpec((1, d), lambda i: (0, 0))],
        out_specs=pl.BlockSpec((tm, d), lambda i: (i, 0)),
        out_shape=jax.ShapeDtypeStruct((m, d), out_dtype),
        compiler_params=_cparams(("parallel",)),
        name="rmsnorm",
    )(x, g)


def _proj_nt_kernel(a_ref, w_ref, o_ref, wb_ref):
    @pl.when(pl.program_id(1) == 0)
    def _():
        wb_ref[...] = w_ref[0].astype(BF16)

    o_ref[...] = _mm_nt(a_ref[...], wb_ref[...]).astype(o_ref.dtype)


def _proj_nt(a, wt_all, layer, row0, n_rows, out_dtype):
    m, k = a.shape
    assert row0 % SUBLANES == 0 and n_rows % TN == 0
    return pl.pallas_call(
        _proj_nt_kernel,
        grid=(n_rows // TN, m // TM),
        in_specs=[pl.BlockSpec((TM, k), lambda j, i: (i, 0)),
                  pl.BlockSpec((pl.Element(1), pl.Element(TN), pl.Element(k)),
                               lambda j, i: (layer, pl.multiple_of(row0 + j * TN, SUBLANES), 0))],
        out_specs=pl.BlockSpec((TM, TN), lambda j, i: (i, j)),
        out_shape=jax.ShapeDtypeStruct((m, n_rows), out_dtype),
        scratch_shapes=[pltpu.VMEM((TN, k), BF16)],
        compiler_params=_cparams(("parallel", "arbitrary")),
        name="proj_nt",
    )(a, wt_all)


def _small_proj_kernel(a_ref, w_ref, o_ref):
    o_ref[...] = _mm_nt(a_ref[...], w_ref[...])


def _small_proj(a, wt_all, layer, row0):
    m, k = a.shape
    assert row0 % LANES == 0
    return pl.pallas_call(
        _small_proj_kernel,
        grid=(m // TM,),
        in_specs=[pl.BlockSpec((TM, k), lambda i: (i, 0)),
                  pl.BlockSpec((None, LANES, k), lambda i: (layer, row0 // LANES, 0))],
        out_specs=pl.BlockSpec((TM, LANES), lambda i: (i, 0)),
        out_shape=jax.ShapeDtypeStruct((m, LANES), F32),
        compiler_params=_cparams(("parallel",)),
        name="small_proj",
    )(a, wt_all)


def _split3(x):
    hi = x.astype(BF16)
    r1 = x - hi.astype(F32)
    mid = r1.astype(BF16)
    lo = (r1 - mid.astype(F32)).astype(BF16)
    return hi, mid, lo


def _gate_prep_kernel(ab_ref, alog_ref, dtb_ref, gc_ref, gt_ref):
    seq = ab_ref.shape[1]
    lane = lax.broadcasted_iota(jnp.int32, (CHUNK, LANES), 1)
    ri = lax.broadcasted_iota(jnp.int32, (CHUNK, CHUNK), 0)
    ci = lax.broadcasted_iota(jnp.int32, (CHUNK, CHUNK), 1)
    lower = jnp.where(ci <= ri, 1.0, 0.0).astype(BF16)
    upper = jnp.where(ci >= ri, 1.0, 0.0).astype(BF16)
    neg_a = -jnp.exp(alog_ref[...])
    dtb = dtb_ref[...]
    for n in range(seq // CHUNK):
        ab = ab_ref[0, n * CHUNK:(n + 1) * CHUNK, :]
        z = ab + dtb
        softplus = jnp.maximum(z, 0.0) + jnp.log1p(jnp.exp(-jnp.abs(z)))
        g = neg_a * softplus
        beta = jax.nn.sigmoid(ab)
        pre = jnp.zeros((CHUNK, LANES), F32)
        suf = jnp.zeros((CHUNK, LANES), F32)
        for part in _split3(g):
            pre = pre + jnp.dot(lower, part, preferred_element_type=F32)
            suf = suf + jnp.dot(upper, part, preferred_element_type=F32)
        gsum = jnp.where(lane < GDN_HEADS, pre, suf)
        gc_ref[0, n * CHUNK:(n + 1) * CHUNK, :] = jnp.where(lane < N_DIR * GDN_HEADS, gsum, beta)
        gt_ref[0, n] = gsum.T[0:N_DIR * GDN_HEADS, :]


def _gate_prep(ab, alog, dtb):
    bsz, seq, _ = ab.shape
    nb = seq // CHUNK
    return pl.pallas_call(
        _gate_prep_kernel,
        grid=(bsz,),
        in_specs=[pl.BlockSpec((1, seq, LANES), lambda b: (b, 0, 0)),
                  pl.BlockSpec((1, LANES), lambda b: (0, 0)),
                  pl.BlockSpec((1, LANES), lambda b: (0, 0))],
        out_specs=[pl.BlockSpec((1, seq, LANES), lambda b: (b, 0, 0)),
                   pl.BlockSpec((1, nb, N_DIR * GDN_HEADS, CHUNK), lambda b: (b, 0, 0, 0))],
        out_shape=[jax.ShapeDtypeStruct((bsz, seq, LANES), F32),
                   jax.ShapeDtypeStruct((bsz, nb, N_DIR * GDN_HEADS, CHUNK), F32)],
        compiler_params=_cparams(("parallel",)),
        name="gate_prep",
    )(ab, alog, dtb)


def _conv_silu_rows(xf, idx, w, r0, r1):
    half = QKV_CONV // 2
    acc = None
    for tap in range(QKV_CONV):
        d = tap - half
        term = xf[idx, SUBLANES + r0 + d:SUBLANES + r1 + d, :] * w[tap:tap + 1, :]
        acc = term if acc is None else acc + term
    return acc * jax.nn.sigmoid(acc)


def _unit_tri_inverse(mats, ri, ci):
    eye = jnp.where(ri == ci, 1.0, 0.0)
    base = 16
    same = lambda sz: (ri >> (sz.bit_length() - 1)) == (ci >> (sz.bit_length() - 1))
    bs = [jnp.where(same(base), -a, 0.0) for a in mats]
    rs = [eye + b for b in bs]
    ps = [_mm(b, b) for b in bs]
    for _ in range(2):
        zs = [_mm(p, jnp.concatenate([p, r], axis=1)) for p, r in zip(ps, rs)]
        ps = [z[:, :CHUNK] for z in zs]
        rs = [r + z[:, CHUNK:] for r, z in zip(rs, zs)]
    ts = [r + _mm(p, r) for p, r in zip(ps, rs)]
    sz = base
    while sz < CHUNK:
        sel = jnp.logical_and(same(2 * sz), jnp.logical_not(same(sz)))
        xs = [_mm(t, jnp.where(sel, a, 0.0)) for t, a in zip(ts, mats)]
        ts = [t - _mm(x, t) for t, x in zip(ts, xs)]
        sz *= 2
    return ts


def _gdn_kernel(q_ref, k_ref, v_ref, z_ref, wq_ref, wk_ref, wv_ref, gc_ref, gt_ref, ng_ref, y_ref,
                qs, ks, vs, us, ws, qds, kdts, qks, decs, outs, xf):
    seq = q_ref.shape[1]
    nb = seq // CHUNK
    hps = GDN_HEADS_PER_STEP
    heads = [pl.program_id(1) * hps + hh for hh in range(hps)]

    ri = lax.broadcasted_iota(jnp.int32, (CHUNK, CHUNK), 0)
    ci = lax.broadcasted_iota(jnp.int32, (CHUNK, CHUNK), 1)
    lane = lax.broadcasted_iota(jnp.int32, (CHUNK, LANES), 1)

    for part, src in enumerate((q_ref, k_ref, v_ref)):
        for hh in range(hps):
            idx = part * hps + hh
            xf[idx, 0:SUBLANES, :] = jnp.zeros((SUBLANES, HEAD_DIM), F32)
            xf[idx, SUBLANES + seq:, :] = jnp.zeros((SUBLANES, HEAD_DIM), F32)
            xf[idx, SUBLANES:SUBLANES + seq, :] = src[0, :, hh * HEAD_DIM:(hh + 1) * HEAD_DIM].astype(F32)

    def conv_group(hh, i):
        r0, r1 = i * PREP_GROUP * CHUNK, (i + 1) * PREP_GROUP * CHUNK
        cols = slice(hh * HEAD_DIM, (hh + 1) * HEAD_DIM)
        qc = _conv_silu_rows(xf, hh, wq_ref[:, cols], r0, r1)
        kc = _conv_silu_rows(xf, hps + hh, wk_ref[:, cols], r0, r1)
        vs[hh, r0:r1, :] = _conv_silu_rows(xf, 2 * hps + hh, wv_ref[:, cols], r0, r1)
        qs[hh, r0:r1, :] = (qc * lax.rsqrt(jnp.sum(qc * qc, axis=-1, keepdims=True) + NORM_EPS)
                            * (HEAD_DIM ** -0.5))
        ks[hh, r0:r1, :] = kc * lax.rsqrt(jnp.sum(kc * kc, axis=-1, keepdims=True) + NORM_EPS)

    def prep(hh, i, after_first_stage):
        chains = []
        for j in range(PREP_GROUP):
            n = i * PREP_GROUP + j
            rows = slice(n * CHUNK, (n + 1) * CHUNK)
            q = qs[hh, rows, :]
            k = ks[hh, rows, :]
            v = vs[hh, rows, :]
            gc = gc_ref[0, rows, :]
            pair = []
            for d in range(N_DIR):
                col = d * GDN_HEADS + heads[hh]
                g_col = jnp.sum(jnp.where(lane == col, gc, 0.0), axis=1, keepdims=True)
                beta = jnp.sum(jnp.where(lane == N_DIR * GDN_HEADS + col, gc, 0.0), axis=1, keepdims=True)
                g_row = gt_ref[0, n, pl.ds(col, 1), :]
                incl = (ci <= ri) if d == 0 else (ci >= ri)
                decay = jnp.where(incl, jnp.exp(jnp.where(incl, g_col - g_row, 0.0)), 0.0)
                pair.append(dict(n=n, rows=rows, c=hh * N_DIR + d, d=d, q=q, k=k, v=v, g_col=g_col,
                                 beta=beta, decay=decay, k_beta=k * beta))
            prod = _mm_nt(jnp.concatenate([pair[0]["k_beta"], pair[1]["k_beta"], q], axis=0), k)
            for d, c in enumerate(pair):
                strict = (ci < ri) if d == 0 else (ci > ri)
                c["a"] = jnp.where(strict, prod[d * CHUNK:(d + 1) * CHUNK, :] * c["decay"], 0.0)
                c["qk"] = prod[N_DIR * CHUNK:, :] * c["decay"]
            chains.extend(pair)
        after_first_stage()
        invs = _unit_tri_inverse([c["a"] for c in chains], ri, ci)
        for c, t in zip(chains, invs):
            ch, d, n, rows, g_col, k, q = c["c"], c["d"], c["n"], c["rows"], c["g_col"], c["k"], c["q"]
            eg = jnp.exp(g_col)
            rhs = jnp.concatenate([c["v"] * c["beta"], c["k_beta"] * eg], axis=1)
            sol = rhs + _mm(jnp.where(ri == ci, 0.0, t), rhs)
            us[ch, rows, :] = sol[:, :HEAD_DIM]
            ws[ch, rows, :] = sol[:, HEAD_DIM:].astype(BF16)
            g_last = g_col[CHUNK - 1:CHUNK, :] if d == 0 else g_col[0:1, :]
            qds[ch, rows, :] = (q * eg).astype(BF16)
            kdts[ch, n] = (k * jnp.exp(g_last - g_col)).T.astype(BF16)
            qks[ch, rows, :] = c["qk"].astype(BF16)
            decs[ch, n] = jnp.broadcast_to(jnp.exp(g_last), (SUBLANES, LANES))

    items = [(hh, i) for hh in range(GDN_HEADS_PER_STEP) for i in range(nb // PREP_GROUP)]
    conv_group(*items[0])
    for idx, (hh, i) in enumerate(items):
        nxt = items[idx + 1] if idx + 1 < len(items) else None
        prep(hh, i, (lambda nxt=nxt: conv_group(*nxt)) if nxt else (lambda: None))

    n_chains = GDN_HEADS_PER_STEP * N_DIR

    def scan(s, states):
        chains = range(n_chains)
        ns = [s if c % N_DIR == 0 else nb - 1 - s for c in chains]
        rows = [pl.ds(pl.multiple_of(n * CHUNK, CHUNK), CHUNK) for n in ns]
        st_b = [states[c].astype(BF16) for c in chains]
        ws_st = [jnp.dot(ws[c, rows[c], :], st_b[c], preferred_element_type=F32) for c in chains]
        v_new = [(us[c, rows[c], :] - ws_st[c]).astype(BF16) for c in chains]
        upd = [jnp.dot(kdts[c, ns[c]], v_new[c], preferred_element_type=F32) for c in chains]
        new_states = tuple(states[c] * decs[c, ns[c]][0:1, :] + upd[c] for c in chains)
        for c in chains:
            outs[c, rows[c], :] = (jnp.dot(qds[c, rows[c], :], st_b[c], preferred_element_type=F32)
                                   + jnp.dot(qks[c, rows[c], :], v_new[c], preferred_element_type=F32))
        return new_states

    zero = jnp.zeros((HEAD_DIM, HEAD_DIM), F32)
    lax.fori_loop(0, nb, scan, (zero,) * n_chains)

    for hh in range(GDN_HEADS_PER_STEP):
        cols = slice(hh * HEAD_DIM, (hh + 1) * HEAD_DIM)
        o = outs[hh * N_DIR] + outs[hh * N_DIR + 1]
        zg = z_ref[0, :, cols].astype(F32)
        y = _rms(o, ng_ref[...]) * (zg * jax.nn.sigmoid(zg))
        y_ref[0, :, cols] = y.astype(y_ref.dtype)


def _gdn(qkvz, conv_w_all, layer, gc, gt, norm_g):
    bsz, seq, _ = qkvz.shape
    nb = seq // CHUNK
    hps = GDN_HEADS_PER_STEP
    width = hps * HEAD_DIM
    groups = GDN_HEADS // hps
    n_chains = hps * N_DIR
    blk = lambda part: pl.BlockSpec((1, seq, width), lambda b, g, part=part: (b, 0, part * groups + g))
    wblk = lambda part: pl.BlockSpec((None, QKV_CONV, width),
                                     lambda b, g, part=part: (layer, 0, part * groups + g))
    return pl.pallas_call(
        _gdn_kernel,
        grid=(bsz, groups),
        in_specs=[blk(0), blk(1), blk(2), blk(3), wblk(0), wblk(1), wblk(2),
                  pl.BlockSpec((1, seq, LANES), lambda b, g: (b, 0, 0)),
                  pl.BlockSpec((1, nb, N_DIR * GDN_HEADS, CHUNK), lambda b, g: (b, 0, 0, 0)),
                  pl.BlockSpec((1, HEAD_DIM), lambda b, g: (0, 0))],
        out_specs=pl.BlockSpec((1, seq, width), lambda b, g: (b, 0, g)),
        out_shape=jax.ShapeDtypeStruct((bsz, seq, GDN_WIDTH), BF16),
        scratch_shapes=[pltpu.VMEM((hps, seq, HEAD_DIM), F32),
                        pltpu.VMEM((hps, seq, HEAD_DIM), F32),
                        pltpu.VMEM((hps, seq, HEAD_DIM), F32),
                        pltpu.VMEM((n_chains, seq, HEAD_DIM), F32),
                        pltpu.VMEM((n_chains, seq, HEAD_DIM), BF16),
                        pltpu.VMEM((n_chains, seq, HEAD_DIM), BF16),
                        pltpu.VMEM((n_chains, nb, HEAD_DIM, CHUNK), BF16),
                        pltpu.VMEM((n_chains, seq, CHUNK), BF16),
                        pltpu.VMEM((n_chains, nb, SUBLANES, LANES), F32),
                        pltpu.VMEM((n_chains, seq, HEAD_DIM), F32),
                        pltpu.VMEM((3 * hps, seq + 2 * SUBLANES, HEAD_DIM), F32)],
        compiler_params=_cparams(("parallel", "parallel")),
        name="gdn",
    )(qkvz, qkvz, qkvz, qkvz, conv_w_all, conv_w_all, conv_w_all, gc, gt, norm_g)


def _sgu_kernel(u_ref, v_ref, lng_ref, lnb_ref, ws_ref, bs_ref, y_ref):
    rows = u_ref.shape[1]
    u = jax.nn.gelu(u_ref[0].astype(F32), approximate=True)
    v = jax.nn.gelu(v_ref[0].astype(F32), approximate=True)
    mu = jnp.mean(v, axis=-1, keepdims=True)
    vc = v - mu
    var = jnp.mean(vc * vc, axis=-1, keepdims=True)
    vn = (vc * lax.rsqrt(var + NORM_EPS) * lng_ref[...] + lnb_ref[...]).astype(BF16)
    for blk in range(rows // SGU_BLOCK):
        r = slice(blk * SGU_BLOCK, (blk + 1) * SGU_BLOCK)
        for g in range(SGU_GROUPS):
            c = slice(g * LANES, (g + 1) * LANES)
            s = jnp.dot(ws_ref[g].astype(BF16), vn[r, c], preferred_element_type=F32) + bs_ref[g]
            y_ref[0, r, c] = (u[r, c] * s).astype(y_ref.dtype)


def _sgu(uvg, ln_g, ln_b, w_s_all, layer, b_s, ts=256):
    bsz, seq, _ = uvg.shape
    return pl.pallas_call(
        _sgu_kernel,
        grid=(bsz, seq // ts),
        in_specs=[pl.BlockSpec((1, ts, SGU_WIDTH), lambda b, i: (b, i, 0)),
                  pl.BlockSpec((1, ts, SGU_WIDTH), lambda b, i: (b, i, 1)),
                  pl.BlockSpec((1, SGU_WIDTH), lambda b, i: (0, 0)),
                  pl.BlockSpec((1, SGU_WIDTH), lambda b, i: (0, 0)),
                  pl.BlockSpec((None, SGU_GROUPS, SGU_BLOCK, SGU_BLOCK), lambda b, i: (layer, 0, 0, 0)),
                  pl.BlockSpec((SGU_GROUPS, SGU_BLOCK, LANES), lambda b, i: (0, 0, 0))],
        out_specs=pl.BlockSpec((1, ts, SGU_WIDTH), lambda b, i: (b, i, 0)),
        out_shape=jax.ShapeDtypeStruct((bsz, seq, SGU_WIDTH), BF16),
        compiler_params=_cparams(("parallel", "parallel")),
        name="sgu",
    )(uvg, uvg, ln_g, ln_b, w_s_all, b_s)


def _merge_kernel(ya_ref, yb_ref, wa_ref, wb_ref, ga_ref, gb_ref, o_ref, wa_s, wb_s):
    @pl.when(pl.program_id(1) == 0)
    def _():
        wa_s[...] = wa_ref[...].astype(BF16)
        wb_s[...] = wb_ref[...].astype(BF16)

    pa = jnp.dot(ya_ref[...], wa_s[...], preferred_element_type=F32)
    pb = jnp.dot(yb_ref[...], wb_s[...], preferred_element_type=F32)
    ga = jax.nn.sigmoid(ga_ref[...].astype(F32))
    gb = jax.nn.sigmoid(gb_ref[...].astype(F32))
    o_ref[...] = (ga * pa + gb * pb).astype(o_ref.dtype)


def _merge(ya, yb, wa_all, wb_all, layer, uvg, gate_col):
    m, ka = ya.shape
    kb = yb.shape[1]
    n = wa_all.shape[2]
    nj = n // TN
    ga0 = gate_col // TN
    return pl.pallas_call(
        _merge_kernel,
        grid=(nj, m // TM),
        in_specs=[pl.BlockSpec((TM, ka), lambda j, i: (i, 0)),
                  pl.BlockSpec((TM, kb), lambda j, i: (i, 0)),
                  pl.BlockSpec((None, ka, TN), lambda j, i: (layer, 0, j)),
                  pl.BlockSpec((None, kb, TN), lambda j, i: (layer, 0, j)),
                  pl.BlockSpec((TM, TN), lambda j, i: (i, ga0 + j)),
                  pl.BlockSpec((TM, TN), lambda j, i: (i, ga0 + nj + j))],
        out_specs=pl.BlockSpec((TM, TN), lambda j, i: (i, j)),
        out_shape=jax.ShapeDtypeStruct((m, n), BF16),
        scratch_shapes=[pltpu.VMEM((ka, TN), BF16), pltpu.VMEM((kb, TN), BF16)],
        compiler_params=_cparams(("parallel", "arbitrary")),
        name="merge",
    )(ya, yb, wa_all, wb_all, uvg, uvg)


def _proj_residual_kernel(a_ref, w_ref, x_ref, o_ref, wb_ref):
    @pl.when(pl.program_id(1) == 0)
    def _():
        wb_ref[...] = w_ref[...].astype(BF16)

    o_ref[...] = x_ref[...] + jnp.dot(a_ref[...], wb_ref[...], preferred_element_type=F32)


def _proj_residual(a, w_all, layer, x):
    m, k = a.shape
    n = w_all.shape[2]
    return pl.pallas_call(
        _proj_residual_kernel,
        grid=(n // TN, m // TM),
        in_specs=[pl.BlockSpec((TM, k), lambda j, i: (i, 0)),
                  pl.BlockSpec((None, k, TN), lambda j, i: (layer, 0, j)),
                  pl.BlockSpec((TM, TN), lambda j, i: (i, j))],
        out_specs=pl.BlockSpec((TM, TN), lambda j, i: (i, j)),
        out_shape=jax.ShapeDtypeStruct((m, n), F32),
        scratch_shapes=[pltpu.VMEM((k, TN), BF16)],
        compiler_params=_cparams(("parallel", "arbitrary")),
        name="proj_residual",
    )(a, w_all, x)


FFN_SUB = 256


def _ffn_kernel(x_ref, xp_ref, xn_ref, g_ref, wg_ref, wv_ref, cwg_ref, cwv_ref, cbg_ref, cbv_ref,
                wd_ref, o_ref, h_ref, *, tiles_per_seq):
    i = pl.program_id(0)
    f = pl.program_id(1)
    tm = x_ref.shape[0]
    tf = wg_ref.shape[1]

    @pl.when(f == 0)
    def _():
        g = g_ref[...]
        x = x_ref[...]
        h_ref[0:tm, :] = _rms(x, g).astype(BF16)
        t = i % tiles_per_seq
        hp = jnp.where(t == 0, 0.0, _rms(xp_ref[...], g))
        hn = jnp.where(t == tiles_per_seq - 1, 0.0, _rms(xn_ref[...], g))
        h_ref[tm:tm + 2 * SUBLANES, :] = jnp.concatenate([hp, hn], axis=0).astype(BF16)
        o_ref[...] = x

    h = h_ref[...]
    row = lax.broadcasted_iota(jnp.int32, (SUBLANES, FFN_SUB), 0)

    def conv(w_ref, cw_ref, cb_ref, cs):
        p = jnp.dot(h, w_ref[:, cs].astype(BF16), preferred_element_type=F32)
        main = p[0:tm, :]
        before = p[tm + SUBLANES - 1:tm + SUBLANES, :]
        after = p[tm + SUBLANES:tm + SUBLANES + 1, :]
        down1 = pltpu.roll(main, 1, axis=0)
        up1 = pltpu.roll(main, tm - 1, axis=0)
        prev = jnp.concatenate([jnp.where(row == 0, before, down1[0:SUBLANES, :]),
                                down1[SUBLANES:, :]], axis=0)
        nxt = jnp.concatenate([up1[0:tm - SUBLANES, :],
                               jnp.where(row == SUBLANES - 1, after, up1[tm - SUBLANES:, :])], axis=0)
        cw = cw_ref[:, cs]
        return prev * cw[0:1, :] + main * cw[1:2, :] + nxt * cw[2:3, :] + cb_ref[:, cs]

    def up(c):
        cs = slice(c * FFN_SUB, (c + 1) * FFN_SUB)
        return conv(wg_ref, cwg_ref, cbg_ref, cs), conv(wv_ref, cwv_ref, cbv_ref, cs)

    def down(c, c_gate, c_val):
        cs = slice(c * FFN_SUB, (c + 1) * FFN_SUB)
        act = (c_gate * jax.nn.sigmoid(c_gate) * c_val).astype(BF16)
        o_ref[...] += jnp.dot(act, wd_ref[cs, :].astype(BF16), preferred_element_type=F32)

    n_sub = tf // FFN_SUB
    pending = up(0)
    for c in range(1, n_sub):
        nxt_up = up(c)
        down(c - 1, *pending)
        pending = nxt_up
    down(n_sub - 1, *pending)


def _ffn(x, g, w_up_all, conv_w_all, conv_b_all, w_down_all, layer, seq, tm=1024, tf=512):
    m, d = x.shape
    nf = D_FF // tf
    tiles_per_seq = seq // tm
    hb = tm // SUBLANES
    last = m // SUBLANES - 1
    once = pl.Buffered(1)
    return pl.pallas_call(
        functools.partial(_ffn_kernel, tiles_per_seq=tiles_per_seq),
        grid=(m // tm, nf),
        in_specs=[pl.BlockSpec((tm, d), lambda i, f: (i, 0), pipeline_mode=once),
                  pl.BlockSpec((SUBLANES, d), lambda i, f: (jnp.maximum(i * hb - 1, 0), 0)),
                  pl.BlockSpec((SUBLANES, d), lambda i, f: (jnp.minimum((i + 1) * hb, last), 0)),
                  pl.BlockSpec((1, d), lambda i, f: (0, 0)),
                  pl.BlockSpec((None, d, tf), lambda i, f: (layer, 0, f)),
                  pl.BlockSpec((None, d, tf), lambda i, f, nf=nf: (layer, 0, nf + f)),
                  pl.BlockSpec((None, 3, tf), lambda i, f: (layer, 0, f)),
                  pl.BlockSpec((None, 3, tf), lambda i, f, nf=nf: (layer, 0, nf + f)),
                  pl.BlockSpec((None, 1, tf), lambda i, f: (layer, 0, f)),
                  pl.BlockSpec((None, 1, tf), lambda i, f, nf=nf: (layer, 0, nf + f)),
                  pl.BlockSpec((None, tf, d), lambda i, f: (layer, f, 0))],
        out_specs=pl.BlockSpec((tm, d), lambda i, f: (i, 0), pipeline_mode=once),
        out_shape=jax.ShapeDtypeStruct((m, d), F32),
        scratch_shapes=[pltpu.VMEM((tm + 2 * SUBLANES, d), BF16)],
        compiler_params=_cparams(("parallel", "arbitrary")),
        name="ffn",
    )(x, x, x, g, w_up_all, w_up_all, conv_w_all, conv_w_all, conv_b_all, conv_b_all, w_down_all)


def _pad_lanes(v):
    v = v.reshape(1, -1).astype(F32)
    return jnp.pad(v, ((0, 0), (0, LANES - v.shape[1])))


def kernel(x, norm_mix_g, w_in, qkv_conv_w, a_log, dt_bias, gdn_norm_g, w_branch_a, sgu_ln_g, sgu_ln_b,
           sgu_w, sgu_b, w_branch_b, w_out, norm_ffn_g, w_up, ffn_conv_w, ffn_conv_b, w_down, final_norm_g):
    bsz, seq, d = x.shape
    depth = w_in.shape[0]
    m = bsz * seq
    n_uvg = w_in.shape[2] - COL_UV
    conv_b = ffn_conv_b.reshape(depth, 1, 2 * D_FF)
    w_in_t = jnp.swapaxes(w_in, 1, 2)
    xf = x.reshape(m, d)
    for l in range(depth):
        h = _rmsnorm(xf, norm_mix_g[l].reshape(1, d), BF16)
        qkvz = _proj_nt(h, w_in_t, l, 0, COL_AB, BF16)
        uvg = _proj_nt(h, w_in_t, l, COL_UV, n_uvg, BF16)
        ab = _small_proj(h, w_in_t, l, COL_AB)

        gc, gt = _gate_prep(ab.reshape(bsz, seq, LANES), _pad_lanes(a_log[l]), _pad_lanes(dt_bias[l]))
        y_a = _gdn(qkvz.reshape(bsz, seq, COL_AB), qkv_conv_w, l, gc, gt,
                   gdn_norm_g[l].reshape(1, HEAD_DIM))
        b_s = jnp.broadcast_to(sgu_b[l][:, :, None], (SGU_GROUPS, SGU_BLOCK, LANES))
        y_b = _sgu(uvg.reshape(bsz, seq, n_uvg), sgu_ln_g[l].reshape(1, SGU_WIDTH),
                   sgu_ln_b[l].reshape(1, SGU_WIDTH), sgu_w, l, b_s)

        merged = _merge(y_a.reshape(m, GDN_WIDTH), y_b.reshape(m, SGU_WIDTH),
                        w_branch_a, w_branch_b, l, uvg, 2 * SGU_WIDTH)
        xf = _proj_residual(merged, w_out, l, xf)

        xf = _ffn(xf, norm_ffn_g[l].reshape(1, d), w_up, ffn_conv_w, conv_b, w_down, l, seq)
    out = _rmsnorm(xf, final_norm_g.reshape(1, d), F32)
    return out.reshape(bsz, seq, d)
```

```python
import functools

import jax
import jax.numpy as jnp
from jax import lax
from jax.experimental import pallas as pl
from jax.experimental.pallas import tpu as pltpu

F32 = jnp.float32
BF16 = jnp.bfloat16

D_MODEL = 2048
GDN_HEADS = 8
HEAD_DIM = 128
GDN_WIDTH = GDN_HEADS * HEAD_DIM
QKV_CONV = 5
N_DIR = 2
SGU_GROUPS = 8
SGU_WIDTH = 1024
SGU_BLOCK = 128
D_FF = 5632
NORM_EPS = 1e-6
N_AB = 2 * N_DIR * GDN_HEADS
COL_AB = 4 * GDN_WIDTH
COL_UV = COL_AB + N_AB

LANES = 128
SUBLANES = 8
GDN_HEADS_PER_STEP = 2
PREP_GROUP = 8
CHUNK = 128
VMEM_LIMIT = 56 * 1024 * 1024
TM = 1024
TN = 1024


def _cparams(sem):
    return pltpu.CompilerParams(dimension_semantics=sem, vmem_limit_bytes=VMEM_LIMIT)


def _mm(a, b):
    return jnp.dot(a.astype(BF16), b.astype(BF16), preferred_element_type=F32)


def _mm_nt(a, b):
    return lax.dot_general(a.astype(BF16), b.astype(BF16), (((1,), (1,)), ((), ())),
                           preferred_element_type=F32)


def _rms(x, g):
    return x * lax.rsqrt(jnp.mean(x * x, axis=-1, keepdims=True) + NORM_EPS) * g


def _rmsnorm_kernel(x_ref, g_ref, o_ref):
    o_ref[...] = _rms(x_ref[...], g_ref[...]).astype(o_ref.dtype)


def _rmsnorm(x, g, out_dtype, tm=512):
    m, d = x.shape
    return pl.pallas_call(
        _rmsnorm_kernel,
        grid=(m // tm,),
        in_specs=[pl.BlockSpec((tm, d), lambda i: (i, 0)),
                  pl.BlockSpec((1, d), lambda i: (0, 0))],
        out_specs=pl.BlockSpec((tm, d), lambda i: (i, 0)),
        out_shape=jax.ShapeDtypeStruct((m, d), out_dtype),
        compiler_params=_cparams(("parallel",)),
        name="rmsnorm",
    )(x, g)


def _proj_nt_kernel(a_ref, w_ref, o_ref, wb_ref):
    @pl.when(pl.program_id(1) == 0)
    def _():
        wb_ref[...] = w_ref[0].astype(BF16)

    o_ref[...] = _mm_nt(a_ref[...], wb_ref[...]).astype(o_ref.dtype)


def _proj_nt(a, wt_all, layer, row0, n_rows, out_dtype):
    m, k = a.shape
    assert row0 % SUBLANES == 0 and n_rows % TN == 0
    return pl.pallas_call(
        _proj_nt_kernel,
        grid=(n_rows // TN, m // TM),
        in_specs=[pl.BlockSpec((TM, k), lambda j, i: (i, 0)),
                  pl.BlockSpec((pl.Element(1), pl.Element(TN), pl.Element(k)),
                               lambda j, i: (layer, pl.multiple_of(row0 + j * TN, SUBLANES), 0))],
        out_specs=pl.BlockSpec((TM, TN), lambda j, i: (i, j)),
        out_shape=jax.ShapeDtypeStruct((m, n_rows), out_dtype),
        scratch_shapes=[pltpu.VMEM((TN, k), BF16)],
        compiler_params=_cparams(("parallel", "arbitrary")),
        name="proj_nt",
    )(a, wt_all)


def _small_proj_kernel(a_ref, w_ref, o_ref):
    o_ref[...] = _mm_nt(a_ref[...], w_ref[...])


def _small_proj(a, wt_all, layer, row0):
    m, k = a.shape
    assert row0 % LANES == 0
    return pl.pallas_call(
        _small_proj_kernel,
        grid=(m // TM,),
        in_specs=[pl.BlockSpec((TM, k), lambda i: (i, 0)),
                  pl.BlockSpec((None, LANES, k), lambda i: (layer, row0 // LANES, 0))],
        out_specs=pl.BlockSpec((TM, LANES), lambda i: (i, 0)),
        out_shape=jax.ShapeDtypeStruct((m, LANES), F32),
        compiler_params=_cparams(("parallel",)),
        name="small_proj",
    )(a, wt_all)


def _split3(x):
    hi = x.astype(BF16)
    r1 = x - hi.astype(F32)
    mid = r1.astype(BF16)
    lo = (r1 - mid.astype(F32)).astype(BF16)
    return hi, mid, lo


def _gate_prep_kernel(ab_ref, alog_ref, dtb_ref, gc_ref, gt_ref):
    seq = ab_ref.shape[1]
    lane = lax.broadcasted_iota(jnp.int32, (CHUNK, LANES), 1)
    ri = lax.broadcasted_iota(jnp.int32, (CHUNK, CHUNK), 0)
    ci = lax.broadcasted_iota(jnp.int32, (CHUNK, CHUNK), 1)
    lower = jnp.where(ci <= ri, 1.0, 0.0).astype(BF16)
    upper = jnp.where(ci >= ri, 1.0, 0.0).astype(BF16)
    neg_a = -jnp.exp(alog_ref[...])
    dtb = dtb_ref[...]
    for n in range(seq // CHUNK):
        ab = ab_ref[0, n * CHUNK:(n + 1) * CHUNK, :]
        z = ab + dtb
        softplus = jnp.maximum(z, 0.0) + jnp.log1p(jnp.exp(-jnp.abs(z)))
        g = neg_a * softplus
        beta = jax.nn.sigmoid(ab)
        pre = jnp.zeros((CHUNK, LANES), F32)
        suf = jnp.zeros((CHUNK, LANES), F32)
        for part in _split3(g):
            pre = pre + jnp.dot(lower, part, preferred_element_type=F32)
            suf = suf + jnp.dot(upper, part, preferred_element_type=F32)
        gsum = jnp.where(lane < GDN_HEADS, pre, suf)
        gc_ref[0, n * CHUNK:(n + 1) * CHUNK, :] = jnp.where(lane < N_DIR * GDN_HEADS, gsum, beta)
        gt_ref[0, n] = gsum.T[0:N_DIR * GDN_HEADS, :]


def _gate_prep(ab, alog, dtb):
    bsz, seq, _ = ab.shape
    nb = seq // CHUNK
    return pl.pallas_call(
        _gate_prep_kernel,
        grid=(bsz,),
        in_specs=[pl.BlockSpec((1, seq, LANES), lambda b: (b, 0, 0)),
                  pl.BlockSpec((1, LANES), lambda b: (0, 0)),
                  pl.BlockSpec((1, LANES), lambda b: (0, 0))],
        out_specs=[pl.BlockSpec((1, seq, LANES), lambda b: (b, 0, 0)),
                   pl.BlockSpec((1, nb, N_DIR * GDN_HEADS, CHUNK), lambda b: (b, 0, 0, 0))],
        out_shape=[jax.ShapeDtypeStruct((bsz, seq, LANES), F32),
                   jax.ShapeDtypeStruct((bsz, nb, N_DIR * GDN_HEADS, CHUNK), F32)],
        compiler_params=_cparams(("parallel",)),
        name="gate_prep",
    )(ab, alog, dtb)


def _conv_silu_rows(xf, idx, w, r0, r1):
    half = QKV_CONV // 2
    acc = None
    for tap in range(QKV_CONV):
        d = tap - half
        term = xf[idx, SUBLANES + r0 + d:SUBLANES + r1 + d, :] * w[tap:tap + 1, :]
        acc = term if acc is None else acc + term
    return acc * jax.nn.sigmoid(acc)


def _unit_tri_inverse(mats, ri, ci):
    eye = jnp.where(ri == ci, 1.0, 0.0)
    base = 16
    same = lambda sz: (ri >> (sz.bit_length() - 1)) == (ci >> (sz.bit_length() - 1))
    bs = [jnp.where(same(base), -a, 0.0) for a in mats]
    rs = [eye + b for b in bs]
    ps = [_mm(b, b) for b in bs]
    for _ in range(2):
        zs = [_mm(p, jnp.concatenate([p, r], axis=1)) for p, r in zip(ps, rs)]
        ps = [z[:, :CHUNK] for z in zs]
        rs = [r + z[:, CHUNK:] for r, z in zip(rs, zs)]
    ts = [r + _mm(p, r) for p, r in zip(ps, rs)]
    sz = base
    while sz < CHUNK:
        sel = jnp.logical_and(same(2 * sz), jnp.logical_not(same(sz)))
        xs = [_mm(t, jnp.where(sel, a, 0.0)) for t, a in zip(ts, mats)]
        ts = [t - _mm(x, t) for t, x in zip(ts, xs)]
        sz *= 2
    return ts


def _gdn_kernel(q_ref, k_ref, v_ref, z_ref, wq_ref, wk_ref, wv_ref, gc_ref, gt_ref, ng_ref, y_ref,
                qs, ks, vs, us, ws, qds, kdts, qks, decs, outs, xf):
    seq = q_ref.shape[1]
    nb = seq // CHUNK
    hps = GDN_HEADS_PER_STEP
    heads = [pl.program_id(1) * hps + hh for hh in range(hps)]

    ri = lax.broadcasted_iota(jnp.int32, (CHUNK, CHUNK), 0)
    ci = lax.broadcasted_iota(jnp.int32, (CHUNK, CHUNK), 1)
    lane = lax.broadcasted_iota(jnp.int32, (CHUNK, LANES), 1)

    for part, src in enumerate((q_ref, k_ref, v_ref)):
        for hh in range(hps):
            idx = part * hps + hh
            xf[idx, 0:SUBLANES, :] = jnp.zeros((SUBLANES, HEAD_DIM), F32)
            xf[idx, SUBLANES + seq:, :] = jnp.zeros((SUBLANES, HEAD_DIM), F32)
            xf[idx, SUBLANES:SUBLANES + seq, :] = src[0, :, hh * HEAD_DIM:(hh + 1) * HEAD_DIM].astype(F32)

    def conv_group(hh, i):
        r0, r1 = i * PREP_GROUP * CHUNK, (i + 1) * PREP_GROUP * CHUNK
        cols = slice(hh * HEAD_DIM, (hh + 1) * HEAD_DIM)
        qc = _conv_silu_rows(xf, hh, wq_ref[:, cols], r0, r1)
        kc = _conv_silu_rows(xf, hps + hh, wk_ref[:, cols], r0, r1)
        vs[hh, r0:r1, :] = _conv_silu_rows(xf, 2 * hps + hh, wv_ref[:, cols], r0, r1)
        qs[hh, r0:r1, :] = (qc * lax.rsqrt(jnp.sum(qc * qc, axis=-1, keepdims=True) + NORM_EPS)
                            * (HEAD_DIM ** -0.5))
        ks[hh, r0:r1, :] = kc * lax.rsqrt(jnp.sum(kc * kc, axis=-1, keepdims=True) + NORM_EPS)

    def prep(hh, i, after_first_stage):
        chains = []
        for j in range(PREP_GROUP):
            n = i * PREP_GROUP + j
            rows = slice(n * CHUNK, (n + 1) * CHUNK)
            q = qs[hh, rows, :]
            k = ks[hh, rows, :]
            v = vs[hh, rows, :]
            gc = gc_ref[0, rows, :]
            pair = []
            for d in range(N_DIR):
                col = d * GDN_HEADS + heads[hh]
                g_col = jnp.sum(jnp.where(lane == col, gc, 0.0), axis=1, keepdims=True)
                beta = jnp.sum(jnp.where(lane == N_DIR * GDN_HEADS + col, gc, 0.0), axis=1, keepdims=True)
                g_row = gt_ref[0, n, pl.ds(col, 1), :]
                incl = (ci <= ri) if d == 0 else (ci >= ri)
                decay = jnp.where(incl, jnp.exp(jnp.where(incl, g_col - g_row, 0.0)), 0.0)
                pair.append(dict(n=n, rows=rows, c=hh * N_DIR + d, d=d, q=q, k=k, v=v, g_col=g_col,
                                 beta=beta, decay=decay, k_beta=k * beta))
            prod = _mm_nt(jnp.concatenate([pair[0]["k_beta"], pair[1]["k_beta"], q], axis=0), k)
            for d, c in enumerate(pair):
                strict = (ci < ri) if d == 0 else (ci > ri)
                c["a"] = jnp.where(strict, prod[d * CHUNK:(d + 1) * CHUNK, :] * c["decay"], 0.0)
                c["qk"] = prod[N_DIR * CHUNK:, :] * c["decay"]
            chains.extend(pair)
        after_first_stage()
        invs = _unit_tri_inverse([c["a"] for c in chains], ri, ci)
        for c, t in zip(chains, invs):
            ch, d, n, rows, g_col, k, q = c["c"], c["d"], c["n"], c["rows"], c["g_col"], c["k"], c["q"]
            eg = jnp.exp(g_col)
            rhs = jnp.concatenate([c["v"] * c["beta"], c["k_beta"] * eg], axis=1)
            sol = rhs + _mm(jnp.where(ri == ci, 0.0, t), rhs)
            us[ch, rows, :] = sol[:, :HEAD_DIM]
            ws[ch, rows, :] = sol[:, HEAD_DIM:].astype(BF16)
            g_last = g_col[CHUNK - 1:CHUNK, :] if d == 0 else g_col[0:1, :]
            qds[ch, rows, :] = (q * eg).astype(BF16)
            kdts[ch, n] = (k * jnp.exp(g_last - g_col)).T.astype(BF16)
            qks[ch, rows, :] = c["qk"].astype(BF16)
            decs[ch, n] = jnp.broadcast_to(jnp.exp(g_last), (SUBLANES, LANES))

    items = [(hh, i) for hh in range(GDN_HEADS_PER_STEP) for i in range(nb // PREP_GROUP)]
    conv_group(*items[0])
    for idx, (hh, i) in enumerate(items):
        nxt = items[idx + 1] if idx + 1 < len(items) else None
        prep(hh, i, (lambda nxt=nxt: conv_group(*nxt)) if nxt else (lambda: None))

    n_chains = GDN_HEADS_PER_STEP * N_DIR

    def scan(s, states):
        chains = range(n_chains)
        ns = [s if c % N_DIR == 0 else nb - 1 - s for c in chains]
        rows = [pl.ds(pl.multiple_of(n * CHUNK, CHUNK), CHUNK) for n in ns]
        st_b = [states[c].astype(BF16) for c in chains]
        ws_st = [jnp.dot(ws[c, rows[c], :], st_b[c], preferred_element_type=F32) for c in chains]
        v_new = [(us[c, rows[c], :] - ws_st[c]).astype(BF16) for c in chains]
        upd = [jnp.dot(kdts[c, ns[c]], v_new[c], preferred_element_type=F32) for c in chains]
        new_states = tuple(states[c] * decs[c, ns[c]][0:1, :] + upd[c] for c in chains)
        for c in chains:
            outs[c, rows[c], :] = (jnp.dot(qds[c, rows[c], :], st_b[c], preferred_element_type=F32)
                                   + jnp.dot(qks[c, rows[c], :], v_new[c], preferred_element_type=F32))
        return new_states

    zero = jnp.zeros((HEAD_DIM, HEAD_DIM), F32)
    lax.fori_loop(0, nb, scan, (zero,) * n_chains)

    for hh in range(GDN_HEADS_PER_STEP):
        cols = slice(hh * HEAD_DIM, (hh + 1) * HEAD_DIM)
        o = outs[hh * N_DIR] + outs[hh * N_DIR + 1]
        zg = z_ref[0, :, cols].astype(F32)
        y = _rms(o, ng_ref[...]) * (zg * jax.nn.sigmoid(zg))
        y_ref[0, :, cols] = y.astype(y_ref.dtype)


def _gdn(qkvz, conv_w_all, layer, gc, gt, norm_g):
    bsz, seq, _ = qkvz.shape
    nb = seq // CHUNK
    hps = GDN_HEADS_PER_STEP
    width = hps * HEAD_DIM
    groups = GDN_HEADS // hps
    n_chains = hps * N_DIR
    blk = lambda part: pl.BlockSpec((1, seq, width), lambda b, g, part=part: (b, 0, part * groups + g))
    wblk = lambda part: pl.BlockSpec((None, QKV_CONV, width),
                                     lambda b, g, part=part: (layer, 0, part * groups + g))
    return pl.pallas_call(
        _gdn_kernel,
        grid=(bsz, groups),
        in_specs=[blk(0), blk(1), blk(2), blk(3), wblk(0), wblk(1), wblk(2),
                  pl.BlockSpec((1, seq, LANES), lambda b, g: (b, 0, 0)),
                  pl.BlockSpec((1, nb, N_DIR * GDN_HEADS, CHUNK), lambda b, g: (b, 0, 0, 0)),
                  pl.BlockSpec((1, HEAD_DIM), lambda b, g: (0, 0))],
        out_specs=pl.BlockSpec((1, seq, width), lambda b, g: (b, 0, g)),
        out_shape=jax.ShapeDtypeStruct((bsz, seq, GDN_WIDTH), BF16),
        scratch_shapes=[pltpu.VMEM((hps, seq, HEAD_DIM), F32),
                        pltpu.VMEM((hps, seq, HEAD_DIM), F32),
                        pltpu.VMEM((hps, seq, HEAD_DIM), F32),
                        pltpu.VMEM((n_chains, seq, HEAD_DIM), F32),
                        pltpu.VMEM((n_chains, seq, HEAD_DIM), BF16),
                        pltpu.VMEM((n_chains, seq, HEAD_DIM), BF16),
                        pltpu.VMEM((n_chains, nb, HEAD_DIM, CHUNK), BF16),
                        pltpu.VMEM((n_chains, seq, CHUNK), BF16),
                        pltpu.VMEM((n_chains, nb, SUBLANES, LANES), F32),
                        pltpu.VMEM((n_chains, seq, HEAD_DIM), F32),
                        pltpu.VMEM((3 * hps, seq + 2 * SUBLANES, HEAD_DIM), F32)],
        compiler_params=_cparams(("parallel", "parallel")),
        name="gdn",
    )(qkvz, qkvz, qkvz, qkvz, conv_w_all, conv_w_all, conv_w_all, gc, gt, norm_g)


def _sgu_kernel(u_ref, v_ref, lng_ref, lnb_ref, ws_ref, bs_ref, y_ref):
    rows = u_ref.shape[1]
    u = jax.nn.gelu(u_ref[0].astype(F32), approximate=True)
    v = jax.nn.gelu(v_ref[0].astype(F32), approximate=True)
    mu = jnp.mean(v, axis=-1, keepdims=True)
    vc = v - mu
    var = jnp.mean(vc * vc, axis=-1, keepdims=True)
    vn = (vc * lax.rsqrt(var + NORM_EPS) * lng_ref[...] + lnb_ref[...]).astype(BF16)
    for blk in range(rows // SGU_BLOCK):
        r = slice(blk * SGU_BLOCK, (blk + 1) * SGU_BLOCK)
        for g in range(SGU_GROUPS):
            c = slice(g * LANES, (g + 1) * LANES)
            s = jnp.dot(ws_ref[g].astype(BF16), vn[r, c], preferred_element_type=F32) + bs_ref[g]
            y_ref[0, r, c] = (u[r, c] * s).astype(y_ref.dtype)


def _sgu(uvg, ln_g, ln_b, w_s_all, layer, b_s, ts=256):
    bsz, seq, _ = uvg.shape
    return pl.pallas_call(
        _sgu_kernel,
        grid=(bsz, seq // ts),
        in_specs=[pl.BlockSpec((1, ts, SGU_WIDTH), lambda b, i: (b, i, 0)),
                  pl.BlockSpec((1, ts, SGU_WIDTH), lambda b, i: (b, i, 1)),
                  pl.BlockSpec((1, SGU_WIDTH), lambda b, i: (0, 0)),
                  pl.BlockSpec((1, SGU_WIDTH), lambda b, i: (0, 0)),
                  pl.BlockSpec((None, SGU_GROUPS, SGU_BLOCK, SGU_BLOCK), lambda b, i: (layer, 0, 0, 0)),
                  pl.BlockSpec((SGU_GROUPS, SGU_BLOCK, LANES), lambda b, i: (0, 0, 0))],
        out_specs=pl.BlockSpec((1, ts, SGU_WIDTH), lambda b, i: (b, i, 0)),
        out_shape=jax.ShapeDtypeStruct((bsz, seq, SGU_WIDTH), BF16),
        compiler_params=_cparams(("parallel", "parallel")),
        name="sgu",
    )(uvg, uvg, ln_g, ln_b, w_s_all, b_s)


def _merge_kernel(ya_ref, yb_ref, wa_ref, wb_ref, ga_ref, gb_ref, o_ref, wa_s, wb_s):
    @pl.when(pl.program_id(1) == 0)
    def _():
        wa_s[...] = wa_ref[...].astype(BF16)
        wb_s[...] = wb_ref[...].astype(BF16)

    pa = jnp.dot(ya_ref[...], wa_s[...], preferred_element_type=F32)
    pb = jnp.dot(yb_ref[...], wb_s[...], preferred_element_type=F32)
    ga = jax.nn.sigmoid(ga_ref[...].astype(F32))
    gb = jax.nn.sigmoid(gb_ref[...].astype(F32))
    o_ref[...] = (ga * pa + gb * pb).astype(o_ref.dtype)


def _merge(ya, yb, wa_all, wb_all, layer, uvg, gate_col):
    m, ka = ya.shape
    kb = yb.shape[1]
    n = wa_all.shape[2]
    nj = n // TN
    ga0 = gate_col // TN
    return pl.pallas_call(
        _merge_kernel,
        grid=(nj, m // TM),
        in_specs=[pl.BlockSpec((TM, ka), lambda j, i: (i, 0)),
                  pl.BlockSpec((TM, kb), lambda j, i: (i, 0)),
                  pl.BlockSpec((None, ka, TN), lambda j, i: (layer, 0, j)),
                  pl.BlockSpec((None, kb, TN), lambda j, i: (layer, 0, j)),
                  pl.BlockSpec((TM, TN), lambda j, i: (i, ga0 + j)),
                  pl.BlockSpec((TM, TN), lambda j, i: (i, ga0 + nj + j))],
        out_specs=pl.BlockSpec((TM, TN), lambda j, i: (i, j)),
        out_shape=jax.ShapeDtypeStruct((m, n), BF16),
        scratch_shapes=[pltpu.VMEM((ka, TN), BF16), pltpu.VMEM((kb, TN), BF16)],
        compiler_params=_cparams(("parallel", "arbitrary")),
        name="merge",
    )(ya, yb, wa_all, wb_all, uvg, uvg)


def _proj_residual_kernel(a_ref, w_ref, x_ref, o_ref, wb_ref):
    @pl.when(pl.program_id(1) == 0)
    def _():
        wb_ref[...] = w_ref[...].astype(BF16)

    o_ref[...] = x_ref[...] + jnp.dot(a_ref[...], wb_ref[...], preferred_element_type=F32)


def _proj_residual(a, w_all, layer, x):
    m, k = a.shape
    n = w_all.shape[2]
    return pl.pallas_call(
        _proj_residual_kernel,
        grid=(n // TN, m // TM),
        in_specs=[pl.BlockSpec((TM, k), lambda j, i: (i, 0)),
                  pl.BlockSpec((None, k, TN), lambda j, i: (layer, 0, j)),
                  pl.BlockSpec((TM, TN), lambda j, i: (i, j))],
        out_specs=pl.BlockSpec((TM, TN), lambda j, i: (i, j)),
        out_shape=jax.ShapeDtypeStruct((m, n), F32),
        scratch_shapes=[pltpu.VMEM((k, TN), BF16)],
        compiler_params=_cparams(("parallel", "arbitrary")),
        name="proj_residual",
    )(a, w_all, x)


FFN_SUB = 256


def _ffn_kernel(x_ref, xp_ref, xn_ref, g_ref, wg_ref, wv_ref, cwg_ref, cwv_ref, cbg_ref, cbv_ref,
                wd_ref, o_ref, h_ref, p_ref, *, tiles_per_seq):
    i = pl.program_id(0)
    f = pl.program_id(1)
    tm = x_ref.shape[0]
    tf = wg_ref.shape[1]

    @pl.when(f == 0)
    def _():
        g = g_ref[...]
        x = x_ref[...]
        h_ref[0:tm, :] = _rms(x, g).astype(BF16)
        t = i % tiles_per_seq
        hp = jnp.where(t == 0, 0.0, _rms(xp_ref[...], g))
        hn = jnp.where(t == tiles_per_seq - 1, 0.0, _rms(xn_ref[...], g))
        h_ref[tm:tm + 2 * SUBLANES, :] = jnp.concatenate([hp, hn], axis=0).astype(BF16)
        o_ref[...] = x

    h = h_ref[...]

    def conv(w_ref, cw_ref, cb_ref, cs, slot):
        p = jnp.dot(h, w_ref[:, cs].astype(BF16), preferred_element_type=F32)
        p_ref[slot, 0:SUBLANES, :] = p[tm:tm + SUBLANES, :]
        p_ref[slot, SUBLANES:SUBLANES + tm, :] = p[0:tm, :]
        p_ref[slot, SUBLANES + tm:, :] = p[tm + SUBLANES:, :]
        cw = cw_ref[:, cs]
        return (p_ref[slot, SUBLANES - 1:SUBLANES - 1 + tm, :] * cw[0:1, :]
                + p_ref[slot, SUBLANES:SUBLANES + tm, :] * cw[1:2, :]
                + p_ref[slot, SUBLANES + 1:SUBLANES + 1 + tm, :] * cw[2:3, :] + cb_ref[:, cs])

    def up(c):
        cs = slice(c * FFN_SUB, (c + 1) * FFN_SUB)
        return conv(wg_ref, cwg_ref, cbg_ref, cs, 2 * c), conv(wv_ref, cwv_ref, cbv_ref, cs, 2 * c + 1)

    def down(c, c_gate, c_val):
        cs = slice(c * FFN_SUB, (c + 1) * FFN_SUB)
        act = (c_gate * jax.nn.sigmoid(c_gate) * c_val).astype(BF16)
        o_ref[...] += jnp.dot(act, wd_ref[cs, :].astype(BF16), preferred_element_type=F32)

    n_sub = tf // FFN_SUB
    pending = up(0)
    for c in range(1, n_sub):
        nxt_up = up(c)
        down(c - 1, *pending)
        pending = nxt_up
    down(n_sub - 1, *pending)


def _ffn(x, g, w_up_all, conv_w_all, conv_b_all, w_down_all, layer, seq, tm=1024, tf=512):
    m, d = x.shape
    nf = D_FF // tf
    tiles_per_seq = seq // tm
    hb = tm // SUBLANES
    last = m // SUBLANES - 1
    once = pl.Buffered(1)
    return pl.pallas_call(
        functools.partial(_ffn_kernel, tiles_per_seq=tiles_per_seq),
        grid=(m // tm, nf),
        in_specs=[pl.BlockSpec((tm, d), lambda i, f: (i, 0), pipeline_mode=once),
                  pl.BlockSpec((SUBLANES, d), lambda i, f: (jnp.maximum(i * hb - 1, 0), 0)),
                  pl.BlockSpec((SUBLANES, d), lambda i, f: (jnp.minimum((i + 1) * hb, last), 0)),
                  pl.BlockSpec((1, d), lambda i, f: (0, 0)),
                  pl.BlockSpec((None, d, tf), lambda i, f: (layer, 0, f)),
                  pl.BlockSpec((None, d, tf), lambda i, f, nf=nf: (layer, 0, nf + f)),
                  pl.BlockSpec((None, 3, tf), lambda i, f: (layer, 0, f)),
                  pl.BlockSpec((None, 3, tf), lambda i, f, nf=nf: (layer, 0, nf + f)),
                  pl.BlockSpec((None, 1, tf), lambda i, f: (layer, 0, f)),
                  pl.BlockSpec((None, 1, tf), lambda i, f, nf=nf: (layer, 0, nf + f)),
                  pl.BlockSpec((None, tf, d), lambda i, f: (layer, f, 0))],
        out_specs=pl.BlockSpec((tm, d), lambda i, f: (i, 0), pipeline_mode=once),
        out_shape=jax.ShapeDtypeStruct((m, d), F32),
        scratch_shapes=[pltpu.VMEM((tm + 2 * SUBLANES, d), BF16),
                        pltpu.VMEM((2 * (tf // FFN_SUB), tm + 2 * SUBLANES, FFN_SUB), F32)],
        compiler_params=_cparams(("parallel", "arbitrary")),
        name="ffn",
    )(x, x, x, g, w_up_all, w_up_all, conv_w_all, conv_w_all, conv_b_all, conv_b_all, w_down_all)


def _pad_lanes(v):
    v = v.reshape(1, -1).astype(F32)
    return jnp.pad(v, ((0, 0), (0, LANES - v.shape[1])))


def kernel(x, norm_mix_g, w_in, qkv_conv_w, a_log, dt_bias, gdn_norm_g, w_branch_a, sgu_ln_g, sgu_ln_b,
           sgu_w, sgu_b, w_branch_b, w_out, norm_ffn_g, w_up, ffn_conv_w, ffn_conv_b, w_down, final_norm_g):
    bsz, seq, d = x.shape
    depth = w_in.shape[0]
    m = bsz * seq
    n_uvg = w_in.shape[2] - COL_UV
    conv_b = ffn_conv_b.reshape(depth, 1, 2 * D_FF)
    w_in_t = jnp.swapaxes(w_in, 1, 2)
    xf = x.reshape(m, d)
    for l in range(depth):
        h = _rmsnorm(xf, norm_mix_g[l].reshape(1, d), BF16)
        qkvz = _proj_nt(h, w_in_t, l, 0, COL_AB, BF16)
        uvg = _proj_nt(h, w_in_t, l, COL_UV, n_uvg, BF16)
        ab = _small_proj(h, w_in_t, l, COL_AB)

        gc, gt = _gate_prep(ab.reshape(bsz, seq, LANES), _pad_lanes(a_log[l]), _pad_lanes(dt_bias[l]))
        y_a = _gdn(qkvz.reshape(bsz, seq, COL_AB), qkv_conv_w, l, gc, gt,
                   gdn_norm_g[l].reshape(1, HEAD_DIM))
        b_s = jnp.broadcast_to(sgu_b[l][:, :, None], (SGU_GROUPS, SGU_BLOCK, LANES))
        y_b = _sgu(uvg.reshape(bsz, seq, n_uvg), sgu_ln_g[l].reshape(1, SGU_WIDTH),
                   sgu_ln_b[l].reshape(1, SGU_WIDTH), sgu_w, l, b_s)

        merged = _merge(y_a.reshape(m, GDN_WIDTH), y_b.reshape(m, SGU_WIDTH),
                        w_branch_a, w_branch_b, l, uvg, 2 * SGU_WIDTH)
        xf = _proj_residual(merged, w_out, l, xf)

        xf = _ffn(xf, norm_ffn_g[l].reshape(1, d), w_up, ffn_conv_w, conv_b, w_down, l, seq)
    out = _rmsnorm(xf, final_norm_g.reshape(1, d), F32)
    return out.reshape(bsz, seq, d)
```
